```python
import jax
import jax.numpy as jnp
from jax import lax
import numpy as np

D_MODEL = 2048
BATCH = 4
SEQ = 4096
DEPTH = 2

GRID_W = 64
CTX_LEN = 256
HEAD_DIM = 128
ROPE_THETA = 10000.0
EPS = 1e-6
A_HEADS = 8
A_KV_HEADS = 2
WINDOW = 128
WBLOCK = 128
B_HEADS = 8
Q_LORA = 512
KV_LORA = 512
NOPE_DIM = 128
ROPE_DIM = 64
V_DIM = 128
Q_BLOCK = 128
C_GROUPS = 8
C_GROUP_DIM = 128
CHUNK = 128

A_WIDTH = A_HEADS * HEAD_DIM
A_KV_WIDTH = A_KV_HEADS * HEAD_DIM
B_WIDTH = B_HEADS * V_DIM
C_WIDTH = C_GROUPS * C_GROUP_DIM
IN_SPLITS = (A_WIDTH, A_KV_WIDTH, A_KV_WIDTH, A_WIDTH,
             Q_LORA, KV_LORA, ROPE_DIM, B_WIDTH,
             C_WIDTH, C_WIDTH, C_WIDTH,
             D_MODEL, D_MODEL, D_MODEL)
IN_COLS = sum(IN_SPLITS)
IN_OFFSETS = tuple(sum(IN_SPLITS[:i + 1]) for i in range(len(IN_SPLITS) - 1))

kernel_name = 'hybrid_gated_swa_mla_sgu_prefix_dit'


def rms_norm(x, g):
    xf = x.astype(jnp.float32)
    y = xf * lax.rsqrt(jnp.mean(xf * xf, axis=-1, keepdims=True) + EPS)
    return (y * g.astype(jnp.float32)).astype(x.dtype)


def _rope_1d(x, pos):
    half = x.shape[-1] // 2
    inv = ROPE_THETA ** (-jnp.arange(half, dtype=jnp.float32) / half)
    ang = pos[:, None] * inv[None, :]
    cos = jnp.cos(ang)[:, None, :].astype(x.dtype)
    sin = jnp.sin(ang)[:, None, :].astype(x.dtype)
    x1, x2 = x[..., :half], x[..., half:]
    return jnp.concatenate([x1 * cos - x2 * sin, x1 * sin + x2 * cos], axis=-1)


def axial_rope(x, row, col):
    d2 = x.shape[-1] // 2
    return jnp.concatenate([_rope_1d(x[..., :d2], row), _rope_1d(x[..., d2:], col)], axis=-1)


def window_gqa(q, k, v, kc, vc, sink):
    b, n, h, d = q.shape
    kvh = k.shape[2]
    g = h // kvh
    nb = n // WBLOCK
    scale = d ** -0.5
    qb = q.reshape(b, nb, WBLOCK, kvh, g, d)
    pad = ((0, 0), (WBLOCK, WBLOCK), (0, 0), (0, 0))
    kp = jnp.pad(k, pad).reshape(b, nb + 2, WBLOCK, kvh, d)
    vp = jnp.pad(v, pad).reshape(b, nb + 2, WBLOCK, kvh, d)
    kw = jnp.concatenate([kp[:, :-2], kp[:, 1:-1], kp[:, 2:]], axis=2)
    vw = jnp.concatenate([vp[:, :-2], vp[:, 1:-1], vp[:, 2:]], axis=2)
    s_loc = jnp.einsum('bnqkgd,bnjkd->bnkgqj', qb, kw).astype(jnp.float32) * scale
    qi = jnp.arange(WBLOCK)[:, None]
    kj = jnp.arange(3 * WBLOCK)[None, :]
    rel = kj - qi
    band = (rel >= WBLOCK - WINDOW) & (rel <= WBLOCK + WINDOW)
    jpos = jnp.arange(nb)[:, None] * WBLOCK - WBLOCK + jnp.arange(3 * WBLOCK)[None, :]
    valid = band[None] & ((jpos >= 0) & (jpos < n))[:, None, :]
    s_loc = jnp.where(valid[None, :, None, None], s_loc, -1e30)
    s_ctx = jnp.einsum('bnqkgd,bjkd->bnkgqj', qb, kc).astype(jnp.float32) * scale
    s_sink = jnp.broadcast_to(sink.astype(jnp.float32).reshape(kvh, g)[None, None, :, :, None, None],
                              s_loc.shape[:-1] + (1,))
    p = jax.nn.softmax(jnp.concatenate([s_loc, s_ctx, s_sink], axis=-1), axis=-1).astype(v.dtype)
    nw = 3 * WBLOCK
    nc = kc.shape[1]
    o = (jnp.einsum('bnkgqj,bnjkd->bnqkgd', p[..., :nw], vw)
         + jnp.einsum('bnkgqj,bjkd->bnqkgd', p[..., nw:nw + nc], vc))
    return o.reshape(b, n, h * d)


def context_gqa(q, k, v, sink):
    b, l, h, d = q.shape
    kvh = k.shape[2]
    g = h // kvh
    qg = q.reshape(b, l, kvh, g, d)
    s = jnp.einsum('bqkgd,bjkd->bkgqj', qg, k).astype(jnp.float32) * (d ** -0.5)
    s_sink = jnp.broadcast_to(sink.astype(jnp.float32).reshape(kvh, g)[None, :, :, None, None],
                              s.shape[:-1] + (1,))
    p = jax.nn.softmax(jnp.concatenate([s, s_sink], axis=-1), axis=-1)[..., :l].astype(v.dtype)
    return jnp.einsum('bkgqj,bjkd->bqkgd', p, v).reshape(b, l, h * d)


def mla_project(cq, ckv, g_q, g_kv, w_uq, w_ukv):
    b, n, _ = cq.shape
    q = (rms_norm(cq, g_q) @ w_uq).reshape(b, n, B_HEADS, NOPE_DIM + ROPE_DIM)
    kv = (rms_norm(ckv, g_kv) @ w_ukv).reshape(b, n, B_HEADS, NOPE_DIM + V_DIM)
    return q[..., :NOPE_DIM], q[..., NOPE_DIM:], kv[..., :NOPE_DIM], kv[..., NOPE_DIM:]


def mla_dense(qn, qr, kn, kr, v, knc, krc, vc):
    b, n, h, _ = qn.shape
    nb = n // Q_BLOCK
    scale = (NOPE_DIM + ROPE_DIM) ** -0.5

    def block(qs):
        qn_b, qr_b = qs
        s_lat = jnp.einsum('bqhd,bkhd->bhqk', qn_b, kn) + jnp.einsum('bqhd,bkd->bhqk', qr_b, kr)
        s_ctx = jnp.einsum('bqhd,bkhd->bhqk', qn_b, knc) + jnp.einsum('bqhd,bkd->bhqk', qr_b, krc)
        p = jax.nn.softmax(jnp.concatenate([s_lat, s_ctx], axis=-1).astype(jnp.float32) * scale,
                           axis=-1).astype(v.dtype)
        return (jnp.einsum('bhqk,bkhd->bqhd', p[..., :n], v)
                + jnp.einsum('bhqk,bkhd->bqhd', p[..., n:], vc))

    def to_blocks(t):
        return t.reshape(b, nb, Q_BLOCK, h, t.shape[-1]).swapaxes(0, 1)

    o = lax.map(block, (to_blocks(qn), to_blocks(qr)))
    return o.swapaxes(0, 1).reshape(b, n, h * V_DIM)


def mla_context(qn, qr, kn, kr, v):
    b, l, h, _ = qn.shape
    scale = (NOPE_DIM + ROPE_DIM) ** -0.5
    s = jnp.einsum('bqhd,bkhd->bhqk', qn, kn) + jnp.einsum('bqhd,bkd->bhqk', qr, kr)
    p = jax.nn.softmax(s.astype(jnp.float32) * scale, axis=-1).astype(v.dtype)
    return jnp.einsum('bhqk,bkhd->bqhd', p, v).reshape(b, l, h * V_DIM)


def chunk_sgu(u, v, ln_g, ln_b, w_s, b_s):
    b, n, _ = v.shape
    vf = v.astype(jnp.float32)
    mu = jnp.mean(vf, axis=-1, keepdims=True)
    var = jnp.mean(jnp.square(vf - mu), axis=-1, keepdims=True)
    vn = ((vf - mu) * lax.rsqrt(var + EPS) * ln_g.astype(jnp.float32)
          + ln_b.astype(jnp.float32)).astype(v.dtype)
    vb = vn.reshape(b, n // CHUNK, CHUNK, C_GROUPS, C_GROUP_DIM)
    mixed = jnp.einsum('gpq,bcqgd->bcpgd', w_s, vb) + b_s.T[None, None, :, :, None]
    return u * mixed.reshape(b, n, C_WIDTH)


def merge_branches(ya, za, ga, yb, zb, gb, yc, zc, gc, w_pa, w_pb, w_pc, w_out):
    m = (jax.nn.sigmoid(ga) * ((ya * jax.nn.silu(za)) @ w_pa)
         + jax.nn.sigmoid(gb) * ((yb * jax.nn.silu(zb)) @ w_pb)
         + jax.nn.sigmoid(gc) * ((yc * jax.nn.silu(zc)) @ w_pc))
    return m @ w_out


def hybrid_layer(x, xc, c, c_ctx, row, col, ada_w, ada_b, norm_g, w_in, sink_a, mla_gq, mla_gkv,
                 w_uq, w_ukv, sgu_ln_g, sgu_ln_b, sgu_w, sgu_b, w_pa, w_pb, w_pc, w_out, need_ctx_out):
    b, n, _ = x.shape
    lc = xc.shape[1]
    shift, scale, gate = jnp.split((jax.nn.silu(c) @ ada_w + ada_b)[:, None, :], 3, axis=-1)
    shift_c, scale_c, gate_c = jnp.split(jax.nn.silu(c_ctx) @ ada_w + ada_b, 3)
    h = rms_norm(x, norm_g) * (1 + scale) + shift
    hc = rms_norm(xc, norm_g) * (1 + scale_c) + shift_c
    (aq, ak, av, az, bcq, bckv, bkr, bz, cu, cv, cz, g_a, g_b, g_c) = jnp.split(h @ w_in, IN_OFFSETS, axis=-1)
    (aqc, akc, avc, azc, bcqc, bckvc, bkrc, bzc, cuc, cvc, czc, g_ac, g_bc, g_cc) = jnp.split(
        hc @ w_in, IN_OFFSETS, axis=-1)

    q_a = axial_rope(aq.reshape(b, n, A_HEADS, HEAD_DIM), row, col)
    k_a = axial_rope(ak.reshape(b, n, A_KV_HEADS, HEAD_DIM), row, col)
    v_a = av.reshape(b, n, A_KV_HEADS, HEAD_DIM)
    k_ac = akc.reshape(b, lc, A_KV_HEADS, HEAD_DIM)
    v_ac = avc.reshape(b, lc, A_KV_HEADS, HEAD_DIM)
    y_a = window_gqa(q_a, k_a, v_a, k_ac, v_ac, sink_a)

    qn, qr, kn, vb = mla_project(bcq, bckv, mla_gq, mla_gkv, w_uq, w_ukv)
    qr = axial_rope(qr, row, col)
    kr = axial_rope(bkr[:, :, None, :], row, col)[:, :, 0]
    qnc, qrc, knc, vbc = mla_project(bcqc, bckvc, mla_gq, mla_gkv, w_uq, w_ukv)
    y_b = mla_dense(qn, qr, kn, kr, vb, knc, bkrc, vbc)

    y_c = chunk_sgu(cu, cv, sgu_ln_g, sgu_ln_b, sgu_w, sgu_b)

    x_new = x + gate * merge_branches(y_a, az, g_a, y_b, bz, g_b, y_c, cz, g_c, w_pa, w_pb, w_pc, w_out)

    if need_ctx_out:
        y_ac = context_gqa(aqc.reshape(b, lc, A_HEADS, HEAD_DIM), k_ac, v_ac, sink_a)
        y_bc = mla_context(qnc, qrc, knc, bkrc, vbc)
        y_cc = chunk_sgu(cuc, cvc, sgu_ln_g, sgu_ln_b, sgu_w, sgu_b)
        xc_new = xc + gate_c * merge_branches(y_ac, azc, g_ac, y_bc, bzc, g_bc, y_cc, czc, g_cc,
                                              w_pa, w_pb, w_pc, w_out)
    else:
        xc_new = xc
    return x_new, xc_new


def setup_inputs(seed: int = 0) -> dict:
    key = jax.random.key(seed)
    ks = jax.random.split(key, 24)
    f32 = jnp.float32
    L = DEPTH
    D = D_MODEL

    def nrm(k, shape, s):
        return jax.random.normal(k, shape, f32) * s

    return {
        'x': nrm(ks[0], (BATCH, SEQ, D), 1.0),
        'c': nrm(ks[1], (BATCH, D), 1.0),
        'ctx': nrm(ks[2], (BATCH, CTX_LEN, D), 1.0),
        'c_ctx': nrm(ks[3], (D,), 1.0),
        'ada_w': nrm(ks[4], (L, D, 3 * D), 0.5 * D ** -0.5),
        'ada_b': nrm(ks[5], (L, 3 * D), 0.02),
        'norm_g': 1.0 + nrm(ks[6], (L, D), 0.02),
        'w_in': nrm(ks[7], (L, D, IN_COLS), D ** -0.5),
        'sink_a': nrm(ks[8], (L, A_HEADS), 0.5),
        'mla_gq': 1.0 + nrm(ks[9], (L, Q_LORA), 0.02),
        'mla_gkv': 1.0 + nrm(ks[10], (L, KV_LORA), 0.02),
        'w_uq': nrm(ks[11], (L, Q_LORA, B_HEADS * (NOPE_DIM + ROPE_DIM)), Q_LORA ** -0.5),
        'w_ukv': nrm(ks[12], (L, KV_LORA, B_HEADS * (NOPE_DIM + V_DIM)), KV_LORA ** -0.5),
        'sgu_ln_g': 1.0 + nrm(ks[13], (L, C_WIDTH), 0.02),
        'sgu_ln_b': nrm(ks[14], (L, C_WIDTH), 0.02),
        'sgu_w': nrm(ks[15], (L, C_GROUPS, CHUNK, CHUNK), CHUNK ** -0.5),
        'sgu_b': 1.0 + nrm(ks[16], (L, C_GROUPS, CHUNK), 0.1),
        'w_pa': nrm(ks[17], (L, A_WIDTH, D), A_WIDTH ** -0.5),
        'w_pb': nrm(ks[18], (L, B_WIDTH, D), B_WIDTH ** -0.5),
        'w_pc': nrm(ks[19], (L, C_WIDTH, D), C_WIDTH ** -0.5),
        'w_out': nrm(ks[20], (L, D, D), D ** -0.5),
        'final_g': 1.0 + nrm(ks[21], (D,), 0.02),
    }


def reference(x, c, ctx, c_ctx, ada_w, ada_b, norm_g, w_in, sink_a, mla_gq, mla_gkv, w_uq, w_ukv,
              sgu_ln_g, sgu_ln_b, sgu_w, sgu_b, w_pa, w_pb, w_pc, w_out, final_g):
    n = x.shape[1]
    rows = n // GRID_W
    row = jnp.repeat(jnp.arange(rows, dtype=jnp.float32), GRID_W)
    col = jnp.tile(jnp.arange(GRID_W, dtype=jnp.float32), rows)
    xc = ctx
    for l in range(DEPTH):
        x, xc = hybrid_layer(x, xc, c, c_ctx, row, col, ada_w[l], ada_b[l], norm_g[l], w_in[l],
                             sink_a[l], mla_gq[l], mla_gkv[l], w_uq[l], w_ukv[l], sgu_ln_g[l],
                             sgu_ln_b[l], sgu_w[l], sgu_b[l], w_pa[l], w_pb[l], w_pc[l], w_out[l],
                             l < DEPTH - 1)
    return rms_norm(x, final_g)
```

```python
import functools

import jax
import jax.numpy as jnp
from jax import lax
from jax.experimental import pallas as pl
from jax.experimental.pallas import tpu as pltpu

F32 = jnp.float32
BF16 = jnp.bfloat16

D = 2048
B = 4
SEQ = 4096
DEPTH = 2
GRID_W = 64
CTX = 256
HD = 128
THETA = 10000.0
EPS = 1e-6
A_HEADS = 8
A_KV = 2
A_GROUP = A_HEADS // A_KV
WBLK = 128
B_HEADS = 8
LORA = 512
NOPE = 128
ROPE = 64
VD = 128
C_GROUPS = 8
CHUNK = 128
W1K = 1024

N_LAT = B * SEQ
N_CTX = B * CTX
R = N_LAT + N_CTX

AQ, AZ, BZ, CU, CV, CZ = 0, 1024, 2048, 3072, 4096, 5120
GA, GB, GC = 6144, 8192, 10240
AK, AV, BCQ, BCKV = 12288, 12544, 12800, 13312
NPROJ = 13824

TM = 1024
TN = 1536
NJ = NPROJ // TN
QB_A = 512
TQ_B = 512
TM_MERGE = 256

VMEM_LIMIT = 56 * 1024 * 1024


def _cparams(sem):
    return pltpu.CompilerParams(dimension_semantics=sem, vmem_limit_bytes=VMEM_LIMIT)


def _rope(x, tab_ref, shift):
    return (x * tab_ref[0]
            + pltpu.roll(x, 128 - shift, 1) * tab_ref[1]
            + pltpu.roll(x, shift, 1) * tab_ref[2])


def _nt_dot(a, b):
    return lax.dot_general(a, b, (((1,), (1,)), ((), ())), preferred_element_type=F32)


def _adaln_kernel(c_ref, w_ref, b_ref, o_ref):
    c = c_ref[...]
    a = (c * jax.nn.sigmoid(c)).astype(BF16)
    o_ref[...] = jnp.dot(a, w_ref[...].astype(BF16), preferred_element_type=F32) + b_ref[...]


def _adaln(cc, ada_w, ada_b):
    tn = 768
    return pl.pallas_call(
        _adaln_kernel,
        grid=(DEPTH, 3 * D // tn),
        in_specs=[
            pl.BlockSpec((8, D), lambda l, j: (0, 0)),
            pl.BlockSpec((None, D, tn), lambda l, j: (l, 0, j)),
            pl.BlockSpec((None, 1, tn), lambda l, j: (l, 0, j)),
        ],
        out_specs=pl.BlockSpec((None, 8, tn), lambda l, j: (l, 0, j)),
        out_shape=jax.ShapeDtypeStruct((DEPTH, 8, 3 * D), F32),
        compiler_params=_cparams(("parallel", "parallel")),
        name="adaln",
    )(cc, ada_w, ada_b.reshape(DEPTH, 1, 3 * D))


def _inproj_kernel(x_ref, g_ref, sc_ref, sh_ref, w_ref, wkr_ref, t128_ref, t64_ref,
                   o_ref, kr_ref, h_ref):
    j = pl.program_id(1)

    @pl.when(j == 0)
    def _():
        x = x_ref[...]
        ms = jnp.mean(x * x, axis=-1, keepdims=True)
        h = (x * lax.rsqrt(ms + EPS) * g_ref[...]) * (1.0 + sc_ref[...]) + sh_ref[...]
        hb = h.astype(BF16)
        h_ref[...] = hb
        kr = jnp.dot(hb, wkr_ref[...], preferred_element_type=F32)
        kr_ref[...] = _rope(kr, t64_ref, ROPE // 4)

    def mm():
        return jnp.dot(h_ref[...], w_ref[...], preferred_element_type=F32)

    def store_roped(acc, n_heads, scale):
        for hh in range(n_heads):
            sl = slice(hh * HD, (hh + 1) * HD)
            o_ref[:, sl] = (_rope(acc[:, sl], t128_ref, HD // 4) * scale).astype(BF16)
        o_ref[:, n_heads * HD:] = acc[:, n_heads * HD:].astype(BF16)

    @pl.when(j == AQ // TN)
    def _():
        store_roped(mm(), A_HEADS, HD ** -0.5)

    @pl.when(j == AK // TN)
    def _():
        store_roped(mm(), A_KV, 1.0)

    @pl.when(jnp.logical_and(j != AQ // TN, j != AK // TN))
    def _():
        o_ref[...] = mm().astype(BF16)


def _inproj(xall, norm_g, scale, shift, w_main, w_kr, t128, t64):
    assert AQ % TN == 0 and AK % TN == 0
    tab_idx = lambda i, j: (0, jnp.where(i == N_LAT // TM, SEQ // TM, i % (SEQ // TM)), 0)
    return pl.pallas_call(
        _inproj_kernel,
        grid=(R // TM, NJ),
        in_specs=[
            pl.BlockSpec((TM, D), lambda i, j: (i, 0)),
            pl.BlockSpec((1, D), lambda i, j: (0, 0)),
            pl.BlockSpec((None, 1, D), lambda i, j: (i // (SEQ // TM), 0, 0)),
            pl.BlockSpec((None, 1, D), lambda i, j: (i // (SEQ // TM), 0, 0)),
            pl.BlockSpec((D, TN), lambda i, j: (0, j)),
            pl.BlockSpec((D, 128), lambda i, j: (0, 0)),
            pl.BlockSpec((3, TM, 128), tab_idx),
            pl.BlockSpec((3, TM, 128), tab_idx),
        ],
        out_specs=[
            pl.BlockSpec((TM, TN), lambda i, j: (i, j)),
            pl.BlockSpec((TM, 128), lambda i, j: (i, 0)),
        ],
        out_shape=[
            jax.ShapeDtypeStruct((R, NPROJ), BF16),
            jax.ShapeDtypeStruct((R, 128), F32),
        ],
        scratch_shapes=[pltpu.VMEM((TM, D), BF16)],
        compiler_params=_cparams(("parallel", "arbitrary")),
        name="inproj",
    )(xall, norm_g, scale, shift, w_main, w_kr, t128, t64)


def _mla_proj_kernel(cq_ref, ckv_ref, kr_ref, gq_ref, gkv_ref, wq_ref, wkv_ref, t64_ref,
                     q_ref, k_ref, v_ref):
    def nrm(x_ref, g_ref):
        xf = x_ref[...].astype(F32)
        ms = jnp.mean(xf * xf, axis=-1, keepdims=True)
        return (xf * lax.rsqrt(ms + EPS) * g_ref[...]).astype(BF16)

    qscale = (NOPE + ROPE) ** -0.5
    q = jnp.dot(nrm(cq_ref, gq_ref), wq_ref[...], preferred_element_type=F32)
    kv = jnp.dot(nrm(ckv_ref, gkv_ref), wkv_ref[...], preferred_element_type=F32)
    krb = kr_ref[...].astype(BF16)
    for h in range(B_HEADS):
        lo = h * 2 * HD
        q_ref[:, lo:lo + HD] = (q[:, lo:lo + HD] * qscale).astype(BF16)
        q_ref[:, lo + HD:lo + 2 * HD] = (
            _rope(q[:, lo + HD:lo + 2 * HD], t64_ref, ROPE // 4) * qscale).astype(BF16)
        k_ref[:, lo:lo + HD] = kv[:, h * HD:(h + 1) * HD].astype(BF16)
        k_ref[:, lo + HD:lo + 2 * HD] = krb
    v_ref[...] = kv[:, B_HEADS * HD:].astype(BF16)


def _mla_proj(proj, kr, gq, gkv, wq, wkv, t64):
    tab_idx = lambda i: (0, jnp.where(i == N_LAT // TM, SEQ // TM, i % (SEQ // TM)), 0)
    return pl.pallas_call(
        _mla_proj_kernel,
        grid=(R // TM,),
        in_specs=[
            pl.BlockSpec((TM, LORA), lambda i: (i, BCQ // LORA)),
            pl.BlockSpec((TM, LORA), lambda i: (i, BCKV // LORA)),
            pl.BlockSpec((TM, 128), lambda i: (i, 0)),
            pl.BlockSpec((1, LORA), lambda i: (0, 0)),
            pl.BlockSpec((1, LORA), lambda i: (0, 0)),
            pl.BlockSpec((LORA, 2 * W1K), lambda i: (0, 0)),
            pl.BlockSpec((LORA, 2 * W1K), lambda i: (0, 0)),
            pl.BlockSpec((3, TM, 128), tab_idx),
        ],
        out_specs=[
            pl.BlockSpec((TM, 2 * W1K), lambda i: (i, 0)),
            pl.BlockSpec((TM, 2 * W1K), lambda i: (i, 0)),
            pl.BlockSpec((TM, W1K), lambda i: (i, 0)),
        ],
        out_shape=[
            jax.ShapeDtypeStruct((R, 2 * W1K), BF16),
            jax.ShapeDtypeStruct((R, 2 * W1K), BF16),
            jax.ShapeDtypeStruct((R, W1K), BF16),
        ],
        compiler_params=_cparams(("parallel",)),
        name="mla_proj",
    )(proj, proj, kr, gq, gkv, wq, wkv, t64)


def _stack_heads(q):
    return jnp.concatenate([q[:, g * HD:(g + 1) * HD] for g in range(A_GROUP)], axis=0)


def _unstack_heads(o, rows):
    return jnp.concatenate([o[g * rows:(g + 1) * rows, :] for g in range(A_GROUP)], axis=1)


def _sink_col(sink_ref, kvh, rows):
    return jnp.concatenate(
        [jnp.full((rows, 1), sink_ref[kvh * A_GROUP + g], F32) for g in range(A_GROUP)], axis=0)


def _attn_a_lat_kernel(sink_ref, q_ref, kp_ref, kc_ref, kn_ref, vp_ref, vc_ref, vn_ref,
                       kx_ref, vx_ref, o_ref):
    kvh = pl.program_id(1)
    qb = pl.program_id(2)
    nsub = QB_A // WBLK
    kfull = jnp.concatenate([kp_ref[...], kc_ref[...], kn_ref[...]], axis=0)
    vfull = jnp.concatenate([vp_ref[...], vc_ref[...], vn_ref[...]], axis=0)
    kx = kx_ref[...]
    vx = vx_ref[...]
    sink = _sink_col(sink_ref, kvh, WBLK)
    rows = A_GROUP * WBLK
    ti = lax.broadcasted_iota(jnp.int32, (rows, 3 * WBLK), 0) % WBLK
    kj = lax.broadcasted_iota(jnp.int32, (rows, 3 * WBLK), 1)
    rel = kj - ti
    band = jnp.logical_and(rel >= 0, rel <= 2 * WBLK)
    first = qb == 0
    last = qb == SEQ // QB_A - 1
    for s in range(nsub):
        qs = _stack_heads(q_ref[s * WBLK:(s + 1) * WBLK, :])
        kw = kfull[s * WBLK:(s + 3) * WBLK, :]
        vw = vfull[s * WBLK:(s + 3) * WBLK, :]
        valid = band
        if s == 0:
            valid = jnp.logical_and(valid, jnp.logical_not(jnp.logical_and(kj < WBLK, first)))
        if s == nsub - 1:
            valid = jnp.logical_and(valid, jnp.logical_not(jnp.logical_and(kj >= 2 * WBLK, last)))
        s_loc = jnp.where(valid, _nt_dot(qs, kw), -1e30)
        s_ctx = _nt_dot(qs, kx)
        m = jnp.maximum(jnp.maximum(jnp.max(s_loc, axis=-1, keepdims=True),
                                    jnp.max(s_ctx, axis=-1, keepdims=True)), sink)
        p_loc = jnp.exp(s_loc - m)
        p_ctx = jnp.exp(s_ctx - m)
        den = (jnp.sum(p_loc, axis=-1, keepdims=True) + jnp.sum(p_ctx, axis=-1, keepdims=True)
               + jnp.exp(sink - m))
        o = (jnp.dot(p_loc.astype(BF16), vw, preferred_element_type=F32)
             + jnp.dot(p_ctx.astype(BF16), vx, preferred_element_type=F32)) / den
        o_ref[s * WBLK:(s + 1) * WBLK, :] = _unstack_heads(o, WBLK).astype(BF16)


def _attn_a_lat(sink, proj):
    nqb = SEQ // QB_A
    sub = QB_A // WBLK
    nblk = SEQ // WBLK
    row = lambda b, k, q: b * nqb + q
    prev = lambda b, k, q: b * nblk + jnp.maximum(q * sub - 1, 0)
    nxt = lambda b, k, q: b * nblk + jnp.minimum(q * sub + sub, nblk - 1)
    ctxb = lambda b, k, q: N_LAT // CTX + b
    return pl.pallas_call(
        _attn_a_lat_kernel,
        grid=(B, A_KV, nqb),
        in_specs=[
            pl.BlockSpec(memory_space=pltpu.SMEM),
            pl.BlockSpec((QB_A, A_GROUP * HD), lambda b, k, q: (row(b, k, q), AQ // (A_GROUP * HD) + k)),
            pl.BlockSpec((WBLK, HD), lambda b, k, q: (prev(b, k, q), AK // HD + k)),
            pl.BlockSpec((QB_A, HD), lambda b, k, q: (row(b, k, q), AK // HD + k)),
            pl.BlockSpec((WBLK, HD), lambda b, k, q: (nxt(b, k, q), AK // HD + k)),
            pl.BlockSpec((WBLK, HD), lambda b, k, q: (prev(b, k, q), AV // HD + k)),
            pl.BlockSpec((QB_A, HD), lambda b, k, q: (row(b, k, q), AV // HD + k)),
            pl.BlockSpec((WBLK, HD), lambda b, k, q: (nxt(b, k, q), AV // HD + k)),
            pl.BlockSpec((CTX, HD), lambda b, k, q: (ctxb(b, k, q), AK // HD + k)),
            pl.BlockSpec((CTX, HD), lambda b, k, q: (ctxb(b, k, q), AV // HD + k)),
        ],
        out_specs=pl.BlockSpec((QB_A, A_GROUP * HD), lambda b, k, q: (row(b, k, q), k)),
        out_shape=jax.ShapeDtypeStruct((R, W1K), BF16),
        compiler_params=_cparams(("parallel", "parallel", "parallel")),
        name="attn_a_lat",
    )(sink, proj, proj, proj, proj, proj, proj, proj, proj, proj)


def _attn_a_ctx_kernel(sink_ref, q_ref, kx_ref, vx_ref, y_hbm_ref, o_ref):
    del y_hbm_ref
    kvh = pl.program_id(1)
    qs = _stack_heads(q_ref[...])
    sink = _sink_col(sink_ref, kvh, CTX)
    s_ctx = _nt_dot(qs, kx_ref[...])
    m = jnp.maximum(jnp.max(s_ctx, axis=-1, keepdims=True), sink)
    p = jnp.exp(s_ctx - m)
    den = jnp.sum(p, axis=-1, keepdims=True) + jnp.exp(sink - m)
    o = jnp.dot(p.astype(BF16), vx_ref[...], preferred_element_type=F32) / den
    o_ref[...] = _unstack_heads(o, CTX).astype(BF16)


def _attn_a_ctx(sink, proj, y_a):
    ctxb = lambda b, k: N_LAT // CTX + b
    return pl.pallas_call(
        _attn_a_ctx_kernel,
        grid=(B, A_KV),
        in_specs=[
            pl.BlockSpec(memory_space=pltpu.SMEM),
            pl.BlockSpec((CTX, A_GROUP * HD), lambda b, k: (ctxb(b, k), AQ // (A_GROUP * HD) + k)),
            pl.BlockSpec((CTX, HD), lambda b, k: (ctxb(b, k), AK // HD + k)),
            pl.BlockSpec((CTX, HD), lambda b, k: (ctxb(b, k), AV // HD + k)),
            pl.BlockSpec(memory_space=pl.ANY),
        ],
        out_specs=pl.BlockSpec((CTX, A_GROUP * HD), lambda b, k: (ctxb(b, k), k)),
        out_shape=jax.ShapeDtypeStruct((R, W1K), BF16),
        input_output_aliases={4: 0},
        compiler_params=_cparams(("parallel", "parallel")),
        name="attn_a_ctx",
    )(sink, proj, proj, proj, y_a)


def _attn_b_kernel(*refs, n_src, aliased):
    q_ref = refs[0]
    srcs = refs[1:1 + 2 * n_src]
    o_ref = refs[-1]
    q = q_ref[...]
    scores = [_nt_dot(q, srcs[2 * i][...]) for i in range(n_src)]
    m = functools.reduce(jnp.maximum, [jnp.max(s, axis=-1, keepdims=True) for s in scores])
    den = None
    o = None
    for i, s in enumerate(scores):
        p = jnp.exp(s - m)
        ps = jnp.sum(p, axis=-1, keepdims=True)
        po = jnp.dot(p.astype(BF16), srcs[2 * i + 1][...], preferred_element_type=F32)
        den = ps if den is None else den + ps
        o = po if o is None else o + po
    o_ref[...] = (o / den).astype(BF16)


def _attn_b_lat(qcat, kcat, vb):
    nq = SEQ // TQ_B
    ctxb = lambda b, h, q: N_LAT // CTX + b
    return pl.pallas_call(
        functools.partial(_attn_b_kernel, n_src=2, aliased=False),
        grid=(B, B_HEADS, nq),
        in_specs=[
            pl.BlockSpec((TQ_B, 2 * HD), lambda b, h, q: (b * nq + q, h)),
            pl.BlockSpec((SEQ, 2 * HD), lambda b, h, q: (b, h)),
            pl.BlockSpec((SEQ, VD), lambda b, h, q: (b, h)),
            pl.BlockSpec((CTX, 2 * HD), lambda b, h, q: (ctxb(b, h, q), h)),
            pl.BlockSpec((CTX, VD), lambda b, h, q: (ctxb(b, h, q), h)),
        ],
        out_specs=pl.BlockSpec((TQ_B, VD), lambda b, h, q: (b * nq + q, h)),
        out_shape=jax.ShapeDtypeStruct((R, W1K), BF16),
        compiler_params=_cparams(("parallel", "parallel", "parallel")),
        name="attn_b_lat",
    )(qcat, kcat, vb, kcat, vb)


def _attn_b_ctx(qcat, kcat, vb, y_b):
    ctxb = lambda b, h: N_LAT // CTX + b
    return pl.pallas_call(
        functools.partial(_attn_b_kernel, n_src=1, aliased=True),
        grid=(B, B_HEADS),
        in_specs=[
            pl.BlockSpec((CTX, 2 * HD), lambda b, h: (ctxb(b, h), h)),
            pl.BlockSpec((CTX, 2 * HD), lambda b, h: (ctxb(b, h), h)),
            pl.BlockSpec((CTX, VD), lambda b, h: (ctxb(b, h), h)),
            pl.BlockSpec(memory_space=pl.ANY),
        ],
        out_specs=pl.BlockSpec((CTX, VD), lambda b, h: (ctxb(b, h), h)),
        out_shape=jax.ShapeDtypeStruct((R, W1K), BF16),
        input_output_aliases={3: 0},
        compiler_params=_cparams(("parallel", "parallel")),
        name="attn_b_ctx",
    )(qcat, kcat, vb, y_b)


def _merge_kernel(x_ref, gate_ref, ya_ref, yb_ref, za_ref, zb_ref, cu_ref, cv_ref, cz_ref,
                  ga_ref, gb_ref, gc_ref, lng_ref, lnb_ref, ws_ref, bs_ref,
                  wpa_ref, wpb_ref, wpc_ref, wout_ref, fg_ref, o_ref, *, final):
    tm = x_ref.shape[0]
    nchunk = tm // CHUNK

    def silu(z_ref):
        z = z_ref[...].astype(F32)
        return z * jax.nn.sigmoid(z)

    def branch(y, z_ref, g_ref, w_ref):
        t = (y * silu(z_ref)).astype(BF16)
        return jax.nn.sigmoid(g_ref[...].astype(F32)) * jnp.dot(
            t, w_ref[...], preferred_element_type=F32)

    v = cv_ref[...].astype(F32)
    mu = jnp.mean(v, axis=-1, keepdims=True)
    vc = v - mu
    var = jnp.mean(vc * vc, axis=-1, keepdims=True)
    vn = (vc * lax.rsqrt(var + EPS) * lng_ref[...] + lnb_ref[...]).astype(BF16)
    cols = []
    for g in range(C_GROUPS):
        gs = slice(g * CHUNK, (g + 1) * CHUNK)
        rhs = jnp.concatenate([vn[c * CHUNK:(c + 1) * CHUNK, gs] for c in range(nchunk)], axis=1)
        mx = jnp.dot(ws_ref[g], rhs, preferred_element_type=F32)
        cols.append(jnp.concatenate(
            [mx[:, c * CHUNK:(c + 1) * CHUNK] for c in range(nchunk)], axis=0))
    bias = jnp.concatenate([bs_ref[...]] * nchunk, axis=0)
    yc = cu_ref[...].astype(F32) * (jnp.concatenate(cols, axis=1) + bias)

    m = (branch(ya_ref[...].astype(F32), za_ref, ga_ref, wpa_ref)
         + branch(yb_ref[...].astype(F32), zb_ref, gb_ref, wpb_ref)
         + branch(yc, cz_ref, gc_ref, wpc_ref))
    out = jnp.dot(m.astype(BF16), wout_ref[...], preferred_element_type=F32)
    xn = x_ref[...] + gate_ref[...] * out
    if final:
        ms = jnp.mean(xn * xn, axis=-1, keepdims=True)
        xn = xn * lax.rsqrt(ms + EPS) * fg_ref[...]
    o_ref[...] = xn


def _merge(xall, gate, y_a, y_b, proj, lng, lnb, ws, bs, wpa, wpb, wpc, wout, fg, final):
    tm = TM_MERGE
    rows = N_LAT if final else R
    seg = lambda c: pl.BlockSpec((tm, W1K), lambda i: (i, c // W1K))
    seg2 = lambda c: pl.BlockSpec((tm, D), lambda i: (i, c // D))
    const = lambda shape: pl.BlockSpec(shape, lambda i: (0,) * len(shape),
                                       pipeline_mode=pl.Buffered(1))
    return pl.pallas_call(
        functools.partial(_merge_kernel, final=final),
        grid=(rows // tm,),
        in_specs=[
            pl.BlockSpec((tm, D), lambda i: (i, 0)),
            pl.BlockSpec((None, 1, D), lambda i: (i // (SEQ // tm), 0, 0)),
            seg(0), seg(0),
            seg(AZ), seg(BZ), seg(CU), seg(CV), seg(CZ),
            seg2(GA), seg2(GB), seg2(GC),
            const((1, W1K)), const((1, W1K)),
            const((C_GROUPS, CHUNK, CHUNK)), const((CHUNK, W1K)),
            const((W1K, D)), const((W1K, D)), const((W1K, D)), const((D, D)),
            const((1, D)),
        ],
        out_specs=pl.BlockSpec((tm, D), lambda i: (i, 0)),
        out_shape=jax.ShapeDtypeStruct((rows, D), F32),
        compiler_params=_cparams(("parallel",)),
        name="merge_final" if final else "merge",
    )(xall, gate, y_a, y_b, proj, proj, proj, proj, proj, proj, proj, proj,
      lng, lnb, ws, bs, wpa, wpb, wpc, wout, fg)


def _rope_tables(width):
    t = jnp.arange(SEQ, dtype=jnp.int32)
    row = (t // GRID_W).astype(F32)
    col = (t % GRID_W).astype(F32)
    half = width // 4
    lane = jnp.arange(128, dtype=jnp.int32)
    u = lane % (width // 2)
    inv = THETA ** (-(u % half).astype(F32) / half)
    pos = jnp.where((lane // (width // 2))[None, :] == 0, row[:, None], col[:, None])
    ang = pos * inv[None, :]
    active = (lane < width)[None, :]
    first = (u < half)[None, :]
    cos = jnp.where(active, jnp.cos(ang), 1.0)
    sin = jnp.where(active, jnp.sin(ang), 0.0)
    tab = jnp.stack([cos, jnp.where(first, -sin, 0.0), jnp.where(first, 0.0, sin)])
    ident = jnp.stack([jnp.ones((TM, 128), F32), jnp.zeros((TM, 128), F32),
                       jnp.zeros((TM, 128), F32)])
    return jnp.concatenate([tab, ident], axis=1)


def _split_w_in(w):
    widths = (1024, 256, 256, 1024, 512, 512, 64, 1024, 1024, 1024, 1024, 2048, 2048, 2048)
    offs = [0]
    for wd in widths:
        offs.append(offs[-1] + wd)
    (aq, ak, av, az, bcq, bckv, bkr, bz, cu, cv, cz, ga, gb, gc) = [
        w[:, offs[i]:offs[i + 1]] for i in range(len(widths))]
    main = jnp.concatenate([aq, az, bz, cu, cv, cz, ga, gb, gc, ak, av, bcq, bckv],
                           axis=1).astype(BF16)
    kr = jnp.pad(bkr, ((0, 0), (0, 128 - ROPE))).astype(BF16)
    return main, kr


def kernel(x, c, ctx, c_ctx, ada_w, ada_b, norm_g, w_in, sink_a, mla_gq, mla_gkv, w_uq, w_ukv,
           sgu_ln_g, sgu_ln_b, sgu_w, sgu_b, w_pa, w_pb, w_pc, w_out, final_g):
    assert x.shape == (B, SEQ, D) and ctx.shape == (B, CTX, D)
    xall = jnp.concatenate([x.reshape(N_LAT, D), ctx.reshape(N_CTX, D)], axis=0)
    cc = jnp.concatenate([c, c_ctx[None, :], jnp.zeros((3, D), F32)], axis=0)
    mod = _adaln(cc, ada_w, ada_b)
    t128 = _rope_tables(HD)
    t64 = _rope_tables(ROPE)

    out = None
    for l in range(DEPTH):
        shift = mod[l, :, 0:D].reshape(8, 1, D)
        scale = mod[l, :, D:2 * D].reshape(8, 1, D)
        gate = mod[l, :, 2 * D:].reshape(8, 1, D)
        w_main, w_kr = _split_w_in(w_in[l])
        wq = jnp.pad(w_uq[l].reshape(LORA, B_HEADS, NOPE + ROPE),
                     ((0, 0), (0, 0), (0, 2 * HD - NOPE - ROPE))).reshape(LORA, 2 * W1K).astype(BF16)
        wkv = w_ukv[l].reshape(LORA, B_HEADS, 2, HD).transpose(0, 2, 1, 3).reshape(
            LORA, 2 * W1K).astype(BF16)
        bs = jnp.repeat(sgu_b[l].T, CHUNK, axis=1)

        proj, kr = _inproj(xall, norm_g[l].reshape(1, D), scale, shift, w_main, w_kr, t128, t64)
        qcat, kcat, vb = _mla_proj(proj, kr, mla_gq[l].reshape(1, LORA),
                                   mla_gkv[l].reshape(1, LORA), wq, wkv, t64)
        y_a = _attn_a_lat(sink_a[l], proj)
        y_b = _attn_b_lat(qcat, kcat, vb)
        final = l == DEPTH - 1
        if not final:
            y_a = _attn_a_ctx(sink_a[l], proj, y_a)
            y_b = _attn_b_ctx(qcat, kcat, vb, y_b)
        res = _merge(xall, gate, y_a, y_b, proj,
                     sgu_ln_g[l].reshape(1, W1K), sgu_ln_b[l].reshape(1, W1K),
                     sgu_w[l].astype(BF16), bs,
                     w_pa[l].astype(BF16), w_pb[l].astype(BF16), w_pc[l].astype(BF16),
                     w_out[l].astype(BF16), final_g.reshape(1, D), final)
        if final:
            out = res
        else:
            xall = res
    return out.reshape(B, SEQ, D)
```

```python
import functools

import jax
import jax.numpy as jnp
from jax import lax
from jax.experimental import pallas as pl
from jax.experimental.pallas import tpu as pltpu

F32 = jnp.float32
BF16 = jnp.bfloat16

D = 2048
B = 4
SEQ = 4096
DEPTH = 2
GRID_W = 64
CTX = 256
HD = 128
THETA = 10000.0
EPS = 1e-6
A_HEADS = 8
A_KV = 2
A_GROUP = A_HEADS // A_KV
WBLK = 128
B_HEADS = 8
LORA = 512
NOPE = 128
ROPE = 64
VD = 128
C_GROUPS = 8
CHUNK = 128
W1K = 1024
LOG2E = 1.4426950408889634

N_LAT = B * SEQ
N_CTX = B * CTX
R = N_LAT + N_CTX

AQ, AZ, BZ, CU, CV, CZ = 0, 1024, 2048, 3072, 4096, 5120
GA, GB, GC = 6144, 8192, 10240
AK, AV, BCQ, BCKV = 12288, 12544, 12800, 13312
NPROJ = 13824

TM = 1024
TN = 1536
NJ = NPROJ // TN
QB_A = 512
TQ_B = 512
TM_MERGE = 256

VMEM_LIMIT = 56 * 1024 * 1024


def _cparams(sem):
    return pltpu.CompilerParams(dimension_semantics=sem, vmem_limit_bytes=VMEM_LIMIT)


def _rope(x, tab_ref, shift):
    return (x * tab_ref[0]
            + pltpu.roll(x, 128 - shift, 1) * tab_ref[1]
            + pltpu.roll(x, shift, 1) * tab_ref[2])


def _nt_dot(a, b):
    return lax.dot_general(a, b, (((1,), (1,)), ((), ())), preferred_element_type=F32)


def _adaln_kernel(c_ref, w_ref, b_ref, o_ref):
    c = c_ref[...]
    a = (c * jax.nn.sigmoid(c)).astype(BF16)
    o_ref[...] = jnp.dot(a, w_ref[...].astype(BF16), preferred_element_type=F32) + b_ref[...]


def _adaln(cc, ada_w, ada_b):
    tn = 768
    return pl.pallas_call(
        _adaln_kernel,
        grid=(DEPTH, 3 * D // tn),
        in_specs=[
            pl.BlockSpec((8, D), lambda l, j: (0, 0)),
            pl.BlockSpec((None, D, tn), lambda l, j: (l, 0, j)),
            pl.BlockSpec((None, 1, tn), lambda l, j: (l, 0, j)),
        ],
        out_specs=pl.BlockSpec((None, 8, tn), lambda l, j: (l, 0, j)),
        out_shape=jax.ShapeDtypeStruct((DEPTH, 8, 3 * D), F32),
        compiler_params=_cparams(("parallel", "parallel")),
        name="adaln",
    )(cc, ada_w, ada_b.reshape(DEPTH, 1, 3 * D))


def _inproj_kernel(x_ref, g_ref, sc_ref, sh_ref, w_ref, wkr_ref, t128_ref, t64_ref,
                   o_ref, kr_ref, h_ref):
    j = pl.program_id(1)

    @pl.when(j == 0)
    def _():
        x = x_ref[...]
        ms = jnp.mean(x * x, axis=-1, keepdims=True)
        h = (x * lax.rsqrt(ms + EPS) * g_ref[...]) * (1.0 + sc_ref[...]) + sh_ref[...]
        hb = h.astype(BF16)
        h_ref[...] = hb
        kr = jnp.dot(hb, wkr_ref[...], preferred_element_type=F32)
        kr_ref[...] = _rope(kr, t64_ref, ROPE // 4)

    def mm():
        return jnp.dot(h_ref[...], w_ref[...], preferred_element_type=F32)

    def store_roped(acc, n_heads, scale):
        for hh in range(n_heads):
            sl = slice(hh * HD, (hh + 1) * HD)
            o_ref[:, sl] = (_rope(acc[:, sl], t128_ref, HD // 4) * scale).astype(BF16)
        o_ref[:, n_heads * HD:] = acc[:, n_heads * HD:].astype(BF16)

    @pl.when(j == AQ // TN)
    def _():
        store_roped(mm(), A_HEADS, HD ** -0.5 * LOG2E)

    @pl.when(j == AK // TN)
    def _():
        store_roped(mm(), A_KV, 1.0)

    @pl.when(jnp.logical_and(j != AQ // TN, j != AK // TN))
    def _():
        o_ref[...] = mm().astype(BF16)


def _inproj(xall, norm_g, scale, shift, w_main, w_kr, t128, t64):
    assert AQ % TN == 0 and AK % TN == 0
    tab_idx = lambda i, j: (0, jnp.where(i == N_LAT // TM, SEQ // TM, i % (SEQ // TM)), 0)
    return pl.pallas_call(
        _inproj_kernel,
        grid=(R // TM, NJ),
        in_specs=[
            pl.BlockSpec((TM, D), lambda i, j: (i, 0)),
            pl.BlockSpec((1, D), lambda i, j: (0, 0)),
            pl.BlockSpec((None, 1, D), lambda i, j: (i // (SEQ // TM), 0, 0)),
            pl.BlockSpec((None, 1, D), lambda i, j: (i // (SEQ // TM), 0, 0)),
            pl.BlockSpec((D, TN), lambda i, j: (0, j)),
            pl.BlockSpec((D, 128), lambda i, j: (0, 0)),
            pl.BlockSpec((3, TM, 128), tab_idx),
            pl.BlockSpec((3, TM, 128), tab_idx),
        ],
        out_specs=[
            pl.BlockSpec((TM, TN), lambda i, j: (i, j)),
            pl.BlockSpec((TM, 128), lambda i, j: (i, 0)),
        ],
        out_shape=[
            jax.ShapeDtypeStruct((R, NPROJ), BF16),
            jax.ShapeDtypeStruct((R, 128), F32),
        ],
        scratch_shapes=[pltpu.VMEM((TM, D), BF16)],
        compiler_params=_cparams(("parallel", "arbitrary")),
        name="inproj",
    )(xall, norm_g, scale, shift, w_main, w_kr, t128, t64)


def _mla_proj_kernel(cq_ref, ckv_ref, kr_ref, gq_ref, gkv_ref, wq_ref, wkv_ref, t64_ref,
                     q_ref, k_ref, v_ref):
    def nrm(x_ref, g_ref):
        xf = x_ref[...].astype(F32)
        ms = jnp.mean(xf * xf, axis=-1, keepdims=True)
        return (xf * lax.rsqrt(ms + EPS) * g_ref[...]).astype(BF16)

    qscale = (NOPE + ROPE) ** -0.5 * LOG2E
    q = jnp.dot(nrm(cq_ref, gq_ref), wq_ref[...], preferred_element_type=F32)
    kv = jnp.dot(nrm(ckv_ref, gkv_ref), wkv_ref[...], preferred_element_type=F32)
    krb = kr_ref[...].astype(BF16)
    ones = jnp.ones((q.shape[0], HD), BF16)
    for h in range(B_HEADS):
        lo = h * 2 * HD
        q_ref[:, lo:lo + HD] = (q[:, lo:lo + HD] * qscale).astype(BF16)
        q_ref[:, lo + HD:lo + 2 * HD] = (
            _rope(q[:, lo + HD:lo + 2 * HD], t64_ref, ROPE // 4) * qscale).astype(BF16)
        k_ref[:, lo:lo + HD] = kv[:, h * HD:(h + 1) * HD].astype(BF16)
        k_ref[:, lo + HD:lo + 2 * HD] = krb
        v_ref[:, lo:lo + HD] = kv[:, (B_HEADS + h) * HD:(B_HEADS + h + 1) * HD].astype(BF16)
        v_ref[:, lo + HD:lo + 2 * HD] = ones


def _mla_proj(proj, kr, gq, gkv, wq, wkv, t64):
    tab_idx = lambda i: (0, jnp.where(i == N_LAT // TM, SEQ // TM, i % (SEQ // TM)), 0)
    return pl.pallas_call(
        _mla_proj_kernel,
        grid=(R // TM,),
        in_specs=[
            pl.BlockSpec((TM, LORA), lambda i: (i, BCQ // LORA)),
            pl.BlockSpec((TM, LORA), lambda i: (i, BCKV // LORA)),
            pl.BlockSpec((TM, 128), lambda i: (i, 0)),
            pl.BlockSpec((1, LORA), lambda i: (0, 0)),
            pl.BlockSpec((1, LORA), lambda i: (0, 0)),
            pl.BlockSpec((LORA, 2 * W1K), lambda i: (0, 0)),
            pl.BlockSpec((LORA, 2 * W1K), lambda i: (0, 0)),
            pl.BlockSpec((3, TM, 128), tab_idx),
        ],
        out_specs=[
            pl.BlockSpec((TM, 2 * W1K), lambda i: (i, 0)),
            pl.BlockSpec((TM, 2 * W1K), lambda i: (i, 0)),
            pl.BlockSpec((TM, 2 * W1K), lambda i: (i, 0)),
        ],
        out_shape=[
            jax.ShapeDtypeStruct((R, 2 * W1K), BF16),
            jax.ShapeDtypeStruct((R, 2 * W1K), BF16),
            jax.ShapeDtypeStruct((R, 2 * W1K), BF16),
        ],
        compiler_params=_cparams(("parallel",)),
        name="mla_proj",
    )(proj, proj, kr, gq, gkv, wq, wkv, t64)


def _stack_heads(q):
    return jnp.concatenate([q[:, g * HD:(g + 1) * HD] for g in range(A_GROUP)], axis=0)


def _unstack_heads(o, rows):
    return jnp.concatenate([o[g * rows:(g + 1) * rows, :] for g in range(A_GROUP)], axis=1)


def _sink_col(sink_ref, kvh, rows):
    return jnp.concatenate(
        [jnp.full((rows, HD), sink_ref[kvh * A_GROUP + g] * LOG2E, F32) for g in range(A_GROUP)],
        axis=0)


def _with_ones(v):
    return jnp.concatenate([v, jnp.ones(v.shape, v.dtype)], axis=1)


def _attn_a_lat_kernel(sink_ref, q_ref, kp_ref, kc_ref, kn_ref, vp_ref, vc_ref, vn_ref,
                       kx_ref, vx_ref, o_ref):
    kvh = pl.program_id(1)
    qb = pl.program_id(2)
    nsub = QB_A // WBLK
    kfull = jnp.concatenate([kp_ref[...], kc_ref[...], kn_ref[...]], axis=0)
    vfull = _with_ones(jnp.concatenate([vp_ref[...], vc_ref[...], vn_ref[...]], axis=0))
    kx = kx_ref[...]
    vx = _with_ones(vx_ref[...])
    sink = _sink_col(sink_ref, kvh, WBLK)
    rows = A_GROUP * WBLK
    ti = lax.broadcasted_iota(jnp.int32, (rows, 3 * WBLK), 0) % WBLK
    kj = lax.broadcasted_iota(jnp.int32, (rows, 3 * WBLK), 1)
    rel = kj - ti
    band = jnp.logical_and(rel >= 0, rel <= 2 * WBLK)
    first = qb == 0
    last = qb == SEQ // QB_A - 1
    for s in range(nsub):
        qs = _stack_heads(q_ref[s * WBLK:(s + 1) * WBLK, :])
        kw = kfull[s * WBLK:(s + 3) * WBLK, :]
        vw = vfull[s * WBLK:(s + 3) * WBLK, :]
        valid = band
        if s == 0:
            valid = jnp.logical_and(valid, jnp.logical_not(jnp.logical_and(kj < WBLK, first)))
        if s == nsub - 1:
            valid = jnp.logical_and(valid, jnp.logical_not(jnp.logical_and(kj >= 2 * WBLK, last)))
        s_loc = jnp.where(valid, _nt_dot(qs, kw), -1e30)
        s_ctx = _nt_dot(qs, kx)
        m = jnp.maximum(jnp.maximum(jnp.max(s_loc, axis=-1, keepdims=True),
                                    jnp.max(s_ctx, axis=-1, keepdims=True)), sink)
        p_loc = jnp.exp2(s_loc - m[:, :1]).astype(BF16)
        p_ctx = jnp.exp2(s_ctx - m[:, :1]).astype(BF16)
        res = (jnp.dot(p_loc, vw, preferred_element_type=F32)
               + jnp.dot(p_ctx, vx, preferred_element_type=F32))
        o = res[:, :HD] / (res[:, HD:] + jnp.exp2(sink - m))
        o_ref[s * WBLK:(s + 1) * WBLK, :] = _unstack_heads(o, WBLK).astype(BF16)


def _attn_a_lat(sink, proj):
    nqb = SEQ // QB_A
    sub = QB_A // WBLK
    nblk = SEQ // WBLK
    row = lambda b, k, q: b * nqb + q
    prev = lambda b, k, q: b * nblk + jnp.maximum(q * sub - 1, 0)
    nxt = lambda b, k, q: b * nblk + jnp.minimum(q * sub + sub, nblk - 1)
    ctxb = lambda b, k, q: N_LAT // CTX + b
    return pl.pallas_call(
        _attn_a_lat_kernel,
        grid=(B, A_KV, nqb),
        in_specs=[
            pl.BlockSpec(memory_space=pltpu.SMEM),
            pl.BlockSpec((QB_A, A_GROUP * HD), lambda b, k, q: (row(b, k, q), AQ // (A_GROUP * HD) + k)),
            pl.BlockSpec((WBLK, HD), lambda b, k, q: (prev(b, k, q), AK // HD + k)),
            pl.BlockSpec((QB_A, HD), lambda b, k, q: (row(b, k, q), AK // HD + k)),
            pl.BlockSpec((WBLK, HD), lambda b, k, q: (nxt(b, k, q), AK // HD + k)),
            pl.BlockSpec((WBLK, HD), lambda b, k, q: (prev(b, k, q), AV // HD + k)),
            pl.BlockSpec((QB_A, HD), lambda b, k, q: (row(b, k, q), AV // HD + k)),
            pl.BlockSpec((WBLK, HD), lambda b, k, q: (nxt(b, k, q), AV // HD + k)),
            pl.BlockSpec((CTX, HD), lambda b, k, q: (ctxb(b, k, q), AK // HD + k)),
            pl.BlockSpec((CTX, HD), lambda b, k, q: (ctxb(b, k, q), AV // HD + k)),
        ],
        out_specs=pl.BlockSpec((QB_A, A_GROUP * HD), lambda b, k, q: (row(b, k, q), k)),
        out_shape=jax.ShapeDtypeStruct((R, W1K), BF16),
        compiler_params=_cparams(("parallel", "parallel", "parallel")),
        name="attn_a_lat",
    )(sink, proj, proj, proj, proj, proj, proj, proj, proj, proj)


def _attn_a_ctx_kernel(sink_ref, q_ref, kx_ref, vx_ref, y_hbm_ref, o_ref):
    del y_hbm_ref
    kvh = pl.program_id(1)
    qs = _stack_heads(q_ref[...])
    sink = _sink_col(sink_ref, kvh, CTX)
    s_ctx = _nt_dot(qs, kx_ref[...])
    m = jnp.maximum(jnp.max(s_ctx, axis=-1, keepdims=True), sink)
    p = jnp.exp2(s_ctx - m[:, :1]).astype(BF16)
    res = jnp.dot(p, _with_ones(vx_ref[...]), preferred_element_type=F32)
    o = res[:, :HD] / (res[:, HD:] + jnp.exp2(sink - m))
    o_ref[...] = _unstack_heads(o, CTX).astype(BF16)


def _attn_a_ctx(sink, proj, y_a):
    ctxb = lambda b, k: N_LAT // CTX + b
    return pl.pallas_call(
        _attn_a_ctx_kernel,
        grid=(B, A_KV),
        in_specs=[
            pl.BlockSpec(memory_space=pltpu.SMEM),
            pl.BlockSpec((CTX, A_GROUP * HD), lambda b, k: (ctxb(b, k), AQ // (A_GROUP * HD) + k)),
            pl.BlockSpec((CTX, HD), lambda b, k: (ctxb(b, k), AK // HD + k)),
            pl.BlockSpec((CTX, HD), lambda b, k: (ctxb(b, k), AV // HD + k)),
            pl.BlockSpec(memory_space=pl.ANY),
        ],
        out_specs=pl.BlockSpec((CTX, A_GROUP * HD), lambda b, k: (ctxb(b, k), k)),
        out_shape=jax.ShapeDtypeStruct((R, W1K), BF16),
        input_output_aliases={4: 0},
        compiler_params=_cparams(("parallel", "parallel")),
        name="attn_a_ctx",
    )(sink, proj, proj, proj, y_a)


def _attn_b_lat_kernel(q_ref, kl_ref, kx_ref, vl_ref, vx_ref, o_ref,
                       sa_ref, ma_ref, sb_ref, mb_ref):
    g = pl.program_id(0)

    @pl.when(g == 0)
    def _():
        sb_ref[...] = jnp.zeros(sb_ref.shape, F32)
        mb_ref[...] = jnp.zeros(mb_ref.shape, F32)

    def step(s_cur, m_cur, s_prev, m_prev):
        q = q_ref[...]
        s1 = _nt_dot(q, kl_ref[...])
        s2 = _nt_dot(q, kx_ref[...])
        s_cur[:, :SEQ] = s1
        s_cur[:, SEQ:] = s2
        m_cur[...] = jnp.maximum(jnp.max(s1, axis=-1, keepdims=True),
                                 jnp.max(s2, axis=-1, keepdims=True))
        m = m_prev[...]
        p1 = jnp.exp2(s_prev[:, :SEQ] - m).astype(BF16)
        p2 = jnp.exp2(s_prev[:, SEQ:] - m).astype(BF16)
        res = (jnp.dot(p1, vl_ref[...], preferred_element_type=F32)
               + jnp.dot(p2, vx_ref[...], preferred_element_type=F32))
        o_ref[...] = (res[:, :VD] / res[:, VD:]).astype(BF16)

    @pl.when(g % 2 == 0)
    def _():
        step(sa_ref, ma_ref, sb_ref, mb_ref)

    @pl.when(g % 2 == 1)
    def _():
        step(sb_ref, mb_ref, sa_ref, ma_ref)


def _attn_b_lat(qcat, kcat, vcat):
    nq = SEQ // TQ_B
    per_b = B_HEADS * nq
    n = B * per_b
    cur = lambda g: jnp.minimum(g, n - 1)
    prv = lambda g: jnp.maximum(g - 1, 0)
    bat = lambda t: t // per_b
    head = lambda t: (t // nq) % B_HEADS
    qrow = lambda t: bat(t) * nq + t % nq
    ctxb = lambda t: N_LAT // CTX + bat(t)
    return pl.pallas_call(
        _attn_b_lat_kernel,
        grid=(n + 1,),
        in_specs=[
            pl.BlockSpec((TQ_B, 2 * HD), lambda g: (qrow(cur(g)), head(cur(g)))),
            pl.BlockSpec((SEQ, 2 * HD), lambda g: (bat(cur(g)), head(cur(g)))),
            pl.BlockSpec((CTX, 2 * HD), lambda g: (ctxb(cur(g)), head(cur(g)))),
            pl.BlockSpec((SEQ, 2 * HD), lambda g: (bat(prv(g)), head(prv(g)))),
            pl.BlockSpec((CTX, 2 * HD), lambda g: (ctxb(prv(g)), head(prv(g)))),
        ],
        out_specs=pl.BlockSpec((TQ_B, VD), lambda g: (qrow(prv(g)), head(prv(g)))),
        out_shape=jax.ShapeDtypeStruct((R, W1K), BF16),
        scratch_shapes=[pltpu.VMEM((TQ_B, SEQ + CTX), F32), pltpu.VMEM((TQ_B, 1), F32),
                        pltpu.VMEM((TQ_B, SEQ + CTX), F32), pltpu.VMEM((TQ_B, 1), F32)],
        compiler_params=_cparams(("arbitrary",)),
        name="attn_b_lat",
    )(qcat, kcat, kcat, vcat, vcat)


def _attn_b_ctx_kernel(q_ref, kx_ref, vx_ref, y_hbm_ref, o_ref):
    del y_hbm_ref
    s = _nt_dot(q_ref[...], kx_ref[...])
    p = jnp.exp2(s - jnp.max(s, axis=-1, keepdims=True)).astype(BF16)
    res = jnp.dot(p, vx_ref[...], preferred_element_type=F32)
    o_ref[...] = (res[:, :VD] / res[:, VD:]).astype(BF16)


def _attn_b_ctx(qcat, kcat, vcat, y_b):
    ctxb = lambda b, h: N_LAT // CTX + b
    return pl.pallas_call(
        _attn_b_ctx_kernel,
        grid=(B, B_HEADS),
        in_specs=[
            pl.BlockSpec((CTX, 2 * HD), lambda b, h: (ctxb(b, h), h)),
            pl.BlockSpec((CTX, 2 * HD), lambda b, h: (ctxb(b, h), h)),
            pl.BlockSpec((CTX, 2 * HD), lambda b, h: (ctxb(b, h), h)),
            pl.BlockSpec(memory_space=pl.ANY),
        ],
        out_specs=pl.BlockSpec((CTX, VD), lambda b, h: (ctxb(b, h), h)),
        out_shape=jax.ShapeDtypeStruct((R, W1K), BF16),
        input_output_aliases={3: 0},
        compiler_params=_cparams(("parallel", "parallel")),
        name="attn_b_ctx",
    )(qcat, kcat, vcat, y_b)


def _merge_kernel(x_ref, gate_ref, ya_ref, yb_ref, za_ref, zb_ref, cu_ref, cv_ref, cz_ref,
                  ga_ref, gb_ref, gc_ref, lng_ref, lnb_ref, ws_ref, bs_ref,
                  wpa_ref, wpb_ref, wpc_ref, wout_ref, fg_ref, o_ref, *, final):
    tm = x_ref.shape[0]
    nchunk = tm // CHUNK

    def silu(z_ref):
        z = z_ref[...].astype(F32)
        return z * jax.nn.sigmoid(z)

    def branch(y, z_ref, g_ref, w_ref):
        t = (y * silu(z_ref)).astype(BF16)
        return jax.nn.sigmoid(g_ref[...].astype(F32)) * jnp.dot(
            t, w_ref[...], preferred_element_type=F32)

    v = cv_ref[...].astype(F32)
    mu = jnp.mean(v, axis=-1, keepdims=True)
    vc = v - mu
    var = jnp.mean(vc * vc, axis=-1, keepdims=True)
    vn = (vc * lax.rsqrt(var + EPS) * lng_ref[...] + lnb_ref[...]).astype(BF16)
    cols = []
    for g in range(C_GROUPS):
        gs = slice(g * CHUNK, (g + 1) * CHUNK)
        rhs = jnp.concatenate([vn[c * CHUNK:(c + 1) * CHUNK, gs] for c in range(nchunk)], axis=1)
        mx = jnp.dot(ws_ref[g], rhs, preferred_element_type=F32)
        cols.append(jnp.concatenate(
            [mx[:, c * CHUNK:(c + 1) * CHUNK] for c in range(nchunk)], axis=0))
    bias = jnp.concatenate([bs_ref[...]] * nchunk, axis=0)
    yc = cu_ref[...].astype(F32) * (jnp.concatenate(cols, axis=1) + bias)

    m = (branch(ya_ref[...].astype(F32), za_ref, ga_ref, wpa_ref)
         + branch(yb_ref[...].astype(F32), zb_ref, gb_ref, wpb_ref)
         + branch(yc, cz_ref, gc_ref, wpc_ref))
    out = jnp.dot(m.astype(BF16), wout_ref[...], preferred_element_type=F32)
    xn = x_ref[...] + gate_ref[...] * out
    if final:
        ms = jnp.mean(xn * xn, axis=-1, keepdims=True)
        xn = xn * lax.rsqrt(ms + EPS) * fg_ref[...]
    o_ref[...] = xn


def _merge(xall, gate, y_a, y_b, proj, lng, lnb, ws, bs, wpa, wpb, wpc, wout, fg, final):
    tm = TM_MERGE
    rows = N_LAT if final else R
    seg = lambda c: pl.BlockSpec((tm, W1K), lambda i: (i, c // W1K))
    seg2 = lambda c: pl.BlockSpec((tm, D), lambda i: (i, c // D))
    const = lambda shape: pl.BlockSpec(shape, lambda i: (0,) * len(shape),
                                       pipeline_mode=pl.Buffered(1))
    return pl.pallas_call(
        functools.partial(_merge_kernel, final=final),
        grid=(rows // tm,),
        in_specs=[
            pl.BlockSpec((tm, D), lambda i: (i, 0)),
            pl.BlockSpec((None, 1, D), lambda i: (i // (SEQ // tm), 0, 0)),
            seg(0), seg(0),
            seg(AZ), seg(BZ), seg(CU), seg(CV), seg(CZ),
            seg2(GA), seg2(GB), seg2(GC),
            const((1, W1K)), const((1, W1K)),
            const((C_GROUPS, CHUNK, CHUNK)), const((CHUNK, W1K)),
            const((W1K, D)), const((W1K, D)), const((W1K, D)), const((D, D)),
            const((1, D)),
        ],
        out_specs=pl.BlockSpec((tm, D), lambda i: (i, 0)),
        out_shape=jax.ShapeDtypeStruct((rows, D), F32),
        compiler_params=_cparams(("parallel",)),
        name="merge_final" if final else "merge",
    )(xall, gate, y_a, y_b, proj, proj, proj, proj, proj, proj, proj, proj,
      lng, lnb, ws, bs, wpa, wpb, wpc, wout, fg)


def _rope_tables(width):
    t = jnp.arange(SEQ, dtype=jnp.int32)
    row = (t // GRID_W).astype(F32)
    col = (t % GRID_W).astype(F32)
    half = width // 4
    lane = jnp.arange(128, dtype=jnp.int32)
    u = lane % (width // 2)
    inv = THETA ** (-(u % half).astype(F32) / half)
    pos = jnp.where((lane // (width // 2))[None, :] == 0, row[:, None], col[:, None])
    ang = pos * inv[None, :]
    active = (lane < width)[None, :]
    first = (u < half)[None, :]
    cos = jnp.where(active, jnp.cos(ang), 1.0)
    sin = jnp.where(active, jnp.sin(ang), 0.0)
    tab = jnp.stack([cos, jnp.where(first, -sin, 0.0), jnp.where(first, 0.0, sin)])
    ident = jnp.stack([jnp.ones((TM, 128), F32), jnp.zeros((TM, 128), F32),
                       jnp.zeros((TM, 128), F32)])
    return jnp.concatenate([tab, ident], axis=1)


def _split_w_in(w):
    widths = (1024, 256, 256, 1024, 512, 512, 64, 1024, 1024, 1024, 1024, 2048, 2048, 2048)
    offs = [0]
    for wd in widths:
        offs.append(offs[-1] + wd)
    (aq, ak, av, az, bcq, bckv, bkr, bz, cu, cv, cz, ga, gb, gc) = [
        w[:, offs[i]:offs[i + 1]] for i in range(len(widths))]
    main = jnp.concatenate([aq, az, bz, cu, cv, cz, ga, gb, gc, ak, av, bcq, bckv],
                           axis=1).astype(BF16)
    kr = jnp.pad(bkr, ((0, 0), (0, 128 - ROPE))).astype(BF16)
    return main, kr


def kernel(x, c, ctx, c_ctx, ada_w, ada_b, norm_g, w_in, sink_a, mla_gq, mla_gkv, w_uq, w_ukv,
           sgu_ln_g, sgu_ln_b, sgu_w, sgu_b, w_pa, w_pb, w_pc, w_out, final_g):
    assert x.shape == (B, SEQ, D) and ctx.shape == (B, CTX, D)
    xall = jnp.concatenate([x.reshape(N_LAT, D), ctx.reshape(N_CTX, D)], axis=0)
    cc = jnp.concatenate([c, c_ctx[None, :], jnp.zeros((3, D), F32)], axis=0)
    mod = _adaln(cc, ada_w, ada_b)
    t128 = _rope_tables(HD)
    t64 = _rope_tables(ROPE)

    out = None
    for l in range(DEPTH):
        shift = mod[l, :, 0:D].reshape(8, 1, D)
        scale = mod[l, :, D:2 * D].reshape(8, 1, D)
        gate = mod[l, :, 2 * D:].reshape(8, 1, D)
        w_main, w_kr = _split_w_in(w_in[l])
        wq = jnp.pad(w_uq[l].reshape(LORA, B_HEADS, NOPE + ROPE),
                     ((0, 0), (0, 0), (0, 2 * HD - NOPE - ROPE))).reshape(LORA, 2 * W1K).astype(BF16)
        wkv = w_ukv[l].reshape(LORA, B_HEADS, 2, HD).transpose(0, 2, 1, 3).reshape(
            LORA, 2 * W1K).astype(BF16)
        bs = jnp.repeat(sgu_b[l].T, CHUNK, axis=1)

        proj, kr = _inproj(xall, norm_g[l].reshape(1, D), scale, shift, w_main, w_kr, t128, t64)
        qcat, kcat, vcat = _mla_proj(proj, kr, mla_gq[l].reshape(1, LORA),
                                   mla_gkv[l].reshape(1, LORA), wq, wkv, t64)
        y_a = _attn_a_lat(sink_a[l], proj)
        y_b = _attn_b_lat(qcat, kcat, vcat)
        final = l == DEPTH - 1
        if not final:
            y_a = _attn_a_ctx(sink_a[l], proj, y_a)
            y_b = _attn_b_ctx(qcat, kcat, vcat, y_b)
        res = _merge(xall, gate, y_a, y_b, proj,
                     sgu_ln_g[l].reshape(1, W1K), sgu_ln_b[l].reshape(1, W1K),
                     sgu_w[l].astype(BF16), bs,
                     w_pa[l].astype(BF16), w_pb[l].astype(BF16), w_pc[l].astype(BF16),
                     w_out[l].astype(BF16), final_g.reshape(1, D), final)
        if final:
            out = res
        else:
            xall = res
    return out.reshape(B, SEQ, D)
```

```python
import functools

import jax
import jax.numpy as jnp
import numpy as np
from jax import lax
from jax.experimental import pallas as pl
from jax.experimental.pallas import tpu as pltpu

F32 = jnp.float32
BF16 = jnp.bfloat16

D = 2048
B = 4
SEQ = 4096
DEPTH = 2
GRID_W = 64
CTX = 256
HD = 128
THETA = 10000.0
EPS = 1e-6
A_HEADS = 8
A_KV = 2
A_GROUP = A_HEADS // A_KV
WBLK = 128
B_HEADS = 8
LORA = 512
NOPE = 128
ROPE = 64
VD = 128
C_GROUPS = 8
CHUNK = 128
W1K = 1024
LOG2E = 1.4426950408889634

N_LAT = B * SEQ
N_CTX = B * CTX
R = N_LAT + N_CTX

AQ, AZ, BZ, CU, CV, CZ = 0, 1024, 2048, 3072, 4096, 5120
GA, GB, GC = 6144, 8192, 10240
AK, AV, BCQ, BCKV = 12288, 12544, 12800, 13312
NPROJ = 13824

TM = 1024
TN = 1536
NJ = NPROJ // TN
QB_A = 512
TQ_B = 1024
TM_MERGE = 256

V7X_VMEM_BYTES = 64 * 1024 * 1024
VMEM_LIMIT = V7X_VMEM_BYTES - 4 * 1024 * 1024


def _cparams(sem):
    return pltpu.CompilerParams(dimension_semantics=sem, vmem_limit_bytes=VMEM_LIMIT)


def _rope(x, tab_ref, shift):
    return (x * tab_ref[0]
            + pltpu.roll(x, 128 - shift, 1) * tab_ref[1]
            + pltpu.roll(x, shift, 1) * tab_ref[2])


def _nt_dot(a, b):
    return lax.dot_general(a, b, (((1,), (1,)), ((), ())), preferred_element_type=F32)


def _adaln_kernel(c_ref, w_ref, b_ref, o_ref):
    c = c_ref[...]
    a = (c * jax.nn.sigmoid(c)).astype(BF16)
    o_ref[...] = jnp.dot(a, w_ref[...].astype(BF16), preferred_element_type=F32) + b_ref[...]


def _adaln(cc, ada_w, ada_b):
    tn = 768
    return pl.pallas_call(
        _adaln_kernel,
        grid=(DEPTH, 3 * D // tn),
        in_specs=[
            pl.BlockSpec((8, D), lambda l, j: (0, 0)),
            pl.BlockSpec((None, D, tn), lambda l, j: (l, 0, j)),
            pl.BlockSpec((None, 1, tn), lambda l, j: (l, 0, j)),
        ],
        out_specs=pl.BlockSpec((None, 8, tn), lambda l, j: (l, 0, j)),
        out_shape=jax.ShapeDtypeStruct((DEPTH, 8, 3 * D), F32),
        compiler_params=_cparams(("parallel", "parallel")),
        name="adaln",
    )(cc, ada_w, ada_b.reshape(DEPTH, 1, 3 * D))


def _inproj_kernel(xl_ref, xc_ref, g_ref, sc_ref, sh_ref, w_ref, wkr_ref, t128_ref, t64_ref,
                   o_ref, kr_ref, h_ref, *, last):
    i = pl.program_id(0)
    j = pl.program_id(1)
    is_ctx = i == N_LAT // TM

    def prologue(x_ref):
        x = x_ref[...]
        ms = jnp.mean(x * x, axis=-1, keepdims=True)
        h = (x * lax.rsqrt(ms + EPS) * g_ref[...]) * (1.0 + sc_ref[...]) + sh_ref[...]
        hb = h.astype(BF16)
        h_ref[...] = hb
        kr = jnp.dot(hb, wkr_ref[...], preferred_element_type=F32)
        kr_ref[...] = _rope(kr, t64_ref, ROPE // 4)

    @pl.when(jnp.logical_and(j == 0, jnp.logical_not(is_ctx)))
    def _():
        prologue(xl_ref)

    @pl.when(jnp.logical_and(j == 0, is_ctx))
    def _():
        prologue(xc_ref)

    def mm():
        return jnp.dot(h_ref[...], w_ref[...], preferred_element_type=F32)

    def store_roped(acc, n_heads, scale):
        for hh in range(n_heads):
            sl = slice(hh * HD, (hh + 1) * HD)
            o_ref[:, sl] = (_rope(acc[:, sl], t128_ref, HD // 4) * scale).astype(BF16)
        o_ref[:, n_heads * HD:] = acc[:, n_heads * HD:].astype(BF16)

    live = jnp.logical_or(jnp.logical_not(is_ctx), j == AK // TN) if last else True

    @pl.when(jnp.logical_and(j == AQ // TN, live))
    def _():
        store_roped(mm(), A_HEADS, HD ** -0.5 * LOG2E)

    @pl.when(j == AK // TN)
    def _():
        store_roped(mm(), A_KV, 1.0)

    @pl.when(jnp.logical_and(jnp.logical_and(j != AQ // TN, j != AK // TN), live))
    def _():
        o_ref[...] = mm().astype(BF16)

    if last:
        @pl.when(jnp.logical_not(live))
        def _():
            o_ref[...] = jnp.zeros(o_ref.shape, BF16)


def _inproj(x_lat, x_ctx, norm_g, scale, shift, w_main, w_kr, t128, t64, last):
    assert AQ % TN == 0 and AK % TN == 0 and N_CTX == TM
    nlt = N_LAT // TM
    tab_idx = lambda i, j: (0, jnp.where(i == nlt, SEQ // TM, i % (SEQ // TM)), 0)
    return pl.pallas_call(
        functools.partial(_inproj_kernel, last=last),
        grid=(R // TM, NJ),
        in_specs=[
            pl.BlockSpec((TM, D), lambda i, j: (jnp.minimum(i, nlt - 1), 0)),
            pl.BlockSpec((TM, D), lambda i, j: (0, 0), pipeline_mode=pl.Buffered(1)),
            pl.BlockSpec((1, D), lambda i, j: (0, 0)),
            pl.BlockSpec((None, 1, D), lambda i, j: (i // (SEQ // TM), 0, 0)),
            pl.BlockSpec((None, 1, D), lambda i, j: (i // (SEQ // TM), 0, 0)),
            pl.BlockSpec((D, TN), lambda i, j: (0, j)),
            pl.BlockSpec((D, 128), lambda i, j: (0, 0)),
            pl.BlockSpec((3, TM, 128), tab_idx, pipeline_mode=pl.Buffered(1)),
            pl.BlockSpec((3, TM, 128), tab_idx, pipeline_mode=pl.Buffered(1)),
        ],
        out_specs=[
            pl.BlockSpec((TM, TN), lambda i, j: (i, j)),
            pl.BlockSpec((TM, 128), lambda i, j: (i, 0)),
        ],
        out_shape=[
            jax.ShapeDtypeStruct((R, NPROJ), BF16),
            jax.ShapeDtypeStruct((R, 128), F32),
        ],
        scratch_shapes=[pltpu.VMEM((TM, D), BF16)],
        compiler_params=_cparams(("parallel", "arbitrary")),
        name="inproj_last" if last else "inproj",
    )(x_lat, x_ctx, norm_g, scale, shift, w_main, w_kr, t128, t64)


def _mla_proj_kernel(cq_ref, ckv_ref, kr_ref, gq_ref, gkv_ref, wq_ref, wkv_ref, t64_ref,
                     q_ref, k_ref, v_ref):
    def nrm(x_ref, g_ref):
        xf = x_ref[...].astype(F32)
        ms = jnp.mean(xf * xf, axis=-1, keepdims=True)
        return (xf * lax.rsqrt(ms + EPS) * g_ref[...]).astype(BF16)

    qscale = (NOPE + ROPE) ** -0.5 * LOG2E
    q = jnp.dot(nrm(cq_ref, gq_ref), wq_ref[...], preferred_element_type=F32)
    kv = jnp.dot(nrm(ckv_ref, gkv_ref), wkv_ref[...], preferred_element_type=F32)
    krb = kr_ref[...].astype(BF16)
    ones = jnp.ones((q.shape[0], HD), BF16)
    for h in range(B_HEADS):
        lo = h * 2 * HD
        q_ref[:, lo:lo + HD] = (q[:, lo:lo + HD] * qscale).astype(BF16)
        q_ref[:, lo + HD:lo + 2 * HD] = (
            _rope(q[:, lo + HD:lo + 2 * HD], t64_ref, ROPE // 4) * qscale).astype(BF16)
        k_ref[:, lo:lo + HD] = kv[:, h * HD:(h + 1) * HD].astype(BF16)
        k_ref[:, lo + HD:lo + 2 * HD] = krb
        v_ref[:, lo:lo + HD] = kv[:, (B_HEADS + h) * HD:(B_HEADS + h + 1) * HD].astype(BF16)
        v_ref[:, lo + HD:lo + 2 * HD] = ones


def _mla_proj(proj, kr, gq, gkv, wq, wkv, t64):
    tab_idx = lambda i: (0, jnp.where(i == N_LAT // TM, SEQ // TM, i % (SEQ // TM)), 0)
    return pl.pallas_call(
        _mla_proj_kernel,
        grid=(R // TM,),
        in_specs=[
            pl.BlockSpec((TM, LORA), lambda i: (i, BCQ // LORA)),
            pl.BlockSpec((TM, LORA), lambda i: (i, BCKV // LORA)),
            pl.BlockSpec((TM, 128), lambda i: (i, 0)),
            pl.BlockSpec((1, LORA), lambda i: (0, 0)),
            pl.BlockSpec((1, LORA), lambda i: (0, 0)),
            pl.BlockSpec((LORA, 2 * W1K), lambda i: (0, 0)),
            pl.BlockSpec((LORA, 2 * W1K), lambda i: (0, 0)),
            pl.BlockSpec((3, TM, 128), tab_idx),
        ],
        out_specs=[
            pl.BlockSpec((TM, 2 * W1K), lambda i: (i, 0)),
            pl.BlockSpec((TM, 2 * W1K), lambda i: (i, 0)),
            pl.BlockSpec((TM, 2 * W1K), lambda i: (i, 0)),
        ],
        out_shape=[
            jax.ShapeDtypeStruct((R, 2 * W1K), BF16),
            jax.ShapeDtypeStruct((R, 2 * W1K), BF16),
            jax.ShapeDtypeStruct((R, 2 * W1K), BF16),
        ],
        compiler_params=_cparams(("parallel",)),
        name="mla_proj",
    )(proj, proj, kr, gq, gkv, wq, wkv, t64)


def _stack_heads(q):
    return jnp.concatenate([q[:, g * HD:(g + 1) * HD] for g in range(A_GROUP)], axis=0)


def _unstack_heads(o, rows):
    return jnp.concatenate([o[g * rows:(g + 1) * rows, :] for g in range(A_GROUP)], axis=1)


def _sink_col(sink_ref, kvh, rows):
    return jnp.concatenate(
        [jnp.full((rows, HD), sink_ref[kvh * A_GROUP + g] * LOG2E, F32) for g in range(A_GROUP)],
        axis=0)


def _with_ones(v):
    return jnp.concatenate([v, jnp.ones(v.shape, v.dtype)], axis=1)


def _attn_a_lat_kernel(sink_ref, q_ref, kp_ref, kc_ref, kn_ref, vp_ref, vc_ref, vn_ref,
                       kx_ref, vx_ref, o_ref):
    kvh = pl.program_id(1)
    qb = pl.program_id(2)
    nsub = QB_A // WBLK
    kfull = jnp.concatenate([kp_ref[...], kc_ref[...], kn_ref[...]], axis=0)
    vfull = _with_ones(jnp.concatenate([vp_ref[...], vc_ref[...], vn_ref[...]], axis=0))
    kx = kx_ref[...]
    vx = _with_ones(vx_ref[...])
    sink = _sink_col(sink_ref, kvh, WBLK)
    rows = A_GROUP * WBLK
    ti = lax.broadcasted_iota(jnp.int32, (rows, 3 * WBLK), 0) % WBLK
    kj = lax.broadcasted_iota(jnp.int32, (rows, 3 * WBLK), 1)
    rel = kj - ti
    band = jnp.logical_and(rel >= 0, rel <= 2 * WBLK)
    first = qb == 0
    last = qb == SEQ // QB_A - 1
    for s in range(nsub):
        qs = _stack_heads(q_ref[s * WBLK:(s + 1) * WBLK, :])
        kw = kfull[s * WBLK:(s + 3) * WBLK, :]
        vw = vfull[s * WBLK:(s + 3) * WBLK, :]
        valid = band
        if s == 0:
            valid = jnp.logical_and(valid, jnp.logical_not(jnp.logical_and(kj < WBLK, first)))
        if s == nsub - 1:
            valid = jnp.logical_and(valid, jnp.logical_not(jnp.logical_and(kj >= 2 * WBLK, last)))
        s_loc = jnp.where(valid, _nt_dot(qs, kw), -1e30)
        s_ctx = _nt_dot(qs, kx)
        m = jnp.maximum(jnp.maximum(jnp.max(s_loc, axis=-1, keepdims=True),
                                    jnp.max(s_ctx, axis=-1, keepdims=True)), sink)
        p_loc = jnp.exp2(s_loc - m[:, :1]).astype(BF16)
        p_ctx = jnp.exp2(s_ctx - m[:, :1]).astype(BF16)
        res = (jnp.dot(p_loc, vw, preferred_element_type=F32)
               + jnp.dot(p_ctx, vx, preferred_element_type=F32))
        o = res[:, :HD] / (res[:, HD:] + jnp.exp2(sink - m))
        o_ref[s * WBLK:(s + 1) * WBLK, :] = _unstack_heads(o, WBLK).astype(BF16)


def _attn_a_lat(sink, proj):
    nqb = SEQ // QB_A
    sub = QB_A // WBLK
    nblk = SEQ // WBLK
    row = lambda b, k, q: b * nqb + q
    prev = lambda b, k, q: b * nblk + jnp.maximum(q * sub - 1, 0)
    nxt = lambda b, k, q: b * nblk + jnp.minimum(q * sub + sub, nblk - 1)
    ctxb = lambda b, k, q: N_LAT // CTX + b
    return pl.pallas_call(
        _attn_a_lat_kernel,
        grid=(B, A_KV, nqb),
        in_specs=[
            pl.BlockSpec(memory_space=pltpu.SMEM),
            pl.BlockSpec((QB_A, A_GROUP * HD), lambda b, k, q: (row(b, k, q), AQ // (A_GROUP * HD) + k)),
            pl.BlockSpec((WBLK, HD), lambda b, k, q: (prev(b, k, q), AK // HD + k)),
            pl.BlockSpec((QB_A, HD), lambda b, k, q: (row(b, k, q), AK // HD + k)),
            pl.BlockSpec((WBLK, HD), lambda b, k, q: (nxt(b, k, q), AK // HD + k)),
            pl.BlockSpec((WBLK, HD), lambda b, k, q: (prev(b, k, q), AV // HD + k)),
            pl.BlockSpec((QB_A, HD), lambda b, k, q: (row(b, k, q), AV // HD + k)),
            pl.BlockSpec((WBLK, HD), lambda b, k, q: (nxt(b, k, q), AV // HD + k)),
            pl.BlockSpec((CTX, HD), lambda b, k, q: (ctxb(b, k, q), AK // HD + k)),
            pl.BlockSpec((CTX, HD), lambda b, k, q: (ctxb(b, k, q), AV // HD + k)),
        ],
        out_specs=pl.BlockSpec((QB_A, A_GROUP * HD), lambda b, k, q: (row(b, k, q), k)),
        out_shape=jax.ShapeDtypeStruct((N_LAT, W1K), BF16),
        compiler_params=_cparams(("parallel", "parallel", "parallel")),
        name="attn_a_lat",
    )(sink, proj, proj, proj, proj, proj, proj, proj, proj, proj)


def _attn_a_ctx_kernel(sink_ref, q_ref, kx_ref, vx_ref, o_ref):
    kvh = pl.program_id(1)
    qs = _stack_heads(q_ref[...])
    sink = _sink_col(sink_ref, kvh, CTX)
    s_ctx = _nt_dot(qs, kx_ref[...])
    m = jnp.maximum(jnp.max(s_ctx, axis=-1, keepdims=True), sink)
    p = jnp.exp2(s_ctx - m[:, :1]).astype(BF16)
    res = jnp.dot(p, _with_ones(vx_ref[...]), preferred_element_type=F32)
    o = res[:, :HD] / (res[:, HD:] + jnp.exp2(sink - m))
    o_ref[...] = _unstack_heads(o, CTX).astype(BF16)


def _attn_a_ctx(sink, proj):
    ctxb = lambda b, k: N_LAT // CTX + b
    return pl.pallas_call(
        _attn_a_ctx_kernel,
        grid=(B, A_KV),
        in_specs=[
            pl.BlockSpec(memory_space=pltpu.SMEM),
            pl.BlockSpec((CTX, A_GROUP * HD), lambda b, k: (ctxb(b, k), AQ // (A_GROUP * HD) + k)),
            pl.BlockSpec((CTX, HD), lambda b, k: (ctxb(b, k), AK // HD + k)),
            pl.BlockSpec((CTX, HD), lambda b, k: (ctxb(b, k), AV // HD + k)),
        ],
        out_specs=pl.BlockSpec((CTX, A_GROUP * HD), lambda b, k: (b, k)),
        out_shape=jax.ShapeDtypeStruct((N_CTX, W1K), BF16),
        compiler_params=_cparams(("parallel", "parallel")),
        name="attn_a_ctx",
    )(sink, proj, proj, proj)


def _attn_b_lat_kernel(q_ref, kl_ref, kx_ref, vl_ref, vx_ref, o_ref,
                       sa_ref, ma_ref, sb_ref, mb_ref):
    g = pl.program_id(0)

    @pl.when(g == 0)
    def _():
        sb_ref[...] = jnp.zeros(sb_ref.shape, F32)
        mb_ref[...] = jnp.zeros(mb_ref.shape, F32)

    def step(s_cur, m_cur, s_prev, m_prev):
        q = q_ref[...]
        s1 = _nt_dot(q, kl_ref[...])
        s2 = _nt_dot(q, kx_ref[...])
        s_cur[:, :SEQ] = s1
        s_cur[:, SEQ:] = s2
        m_cur[...] = jnp.maximum(jnp.max(s1, axis=-1, keepdims=True),
                                 jnp.max(s2, axis=-1, keepdims=True))
        m = m_prev[...]
        p1 = jnp.exp2(s_prev[:, :SEQ] - m).astype(BF16)
        p2 = jnp.exp2(s_prev[:, SEQ:] - m).astype(BF16)
        res = (jnp.dot(p1, vl_ref[...], preferred_element_type=F32)
               + jnp.dot(p2, vx_ref[...], preferred_element_type=F32))
        o_ref[...] = (res[:, :VD] / res[:, VD:]).astype(BF16)

    @pl.when(g % 2 == 0)
    def _():
        step(sa_ref, ma_ref, sb_ref, mb_ref)

    @pl.when(g % 2 == 1)
    def _():
        step(sb_ref, mb_ref, sa_ref, ma_ref)


def _attn_b_lat(qcat, kcat, vcat):
    nq = SEQ // TQ_B
    per_b = B_HEADS * nq
    n = B * per_b
    cur = lambda g: jnp.minimum(g, n - 1)
    prv = lambda g: jnp.maximum(g - 1, 0)
    bat = lambda t: t // per_b
    head = lambda t: (t // nq) % B_HEADS
    qrow = lambda t: bat(t) * nq + t % nq
    ctxb = lambda t: N_LAT // CTX + bat(t)
    return pl.pallas_call(
        _attn_b_lat_kernel,
        grid=(n + 1,),
        in_specs=[
            pl.BlockSpec((TQ_B, 2 * HD), lambda g: (qrow(cur(g)), head(cur(g)))),
            pl.BlockSpec((SEQ, 2 * HD), lambda g: (bat(cur(g)), head(cur(g)))),
            pl.BlockSpec((CTX, 2 * HD), lambda g: (ctxb(cur(g)), head(cur(g)))),
            pl.BlockSpec((SEQ, 2 * HD), lambda g: (bat(prv(g)), head(prv(g)))),
            pl.BlockSpec((CTX, 2 * HD), lambda g: (ctxb(prv(g)), head(prv(g)))),
        ],
        out_specs=pl.BlockSpec((TQ_B, VD), lambda g: (qrow(prv(g)), head(prv(g)))),
        out_shape=jax.ShapeDtypeStruct((N_LAT, W1K), BF16),
        scratch_shapes=[pltpu.VMEM((TQ_B, SEQ + CTX), F32), pltpu.VMEM((TQ_B, 1), F32),
                        pltpu.VMEM((TQ_B, SEQ + CTX), F32), pltpu.VMEM((TQ_B, 1), F32)],
        compiler_params=_cparams(("arbitrary",)),
        name="attn_b_lat",
    )(qcat, kcat, kcat, vcat, vcat)


def _attn_b_ctx_kernel(q_ref, kx_ref, vx_ref, o_ref):
    s =_nt_dot(q_ref[...], kx_ref[...])
    p = jnp.exp2(s - jnp.max(s, axis=-1, keepdims=True)).astype(BF16)
    res = jnp.dot(p, vx_ref[...], preferred_element_type=F32)
    o_ref[...] = (res[:, :VD] / res[:, VD:]).astype(BF16)


def _attn_b_ctx(qcat, kcat, vcat):
    ctxb = lambda b, h: N_LAT // CTX + b
    return pl.pallas_call(
        _attn_b_ctx_kernel,
        grid=(B, B_HEADS),
        in_specs=[
            pl.BlockSpec((CTX, 2 * HD), lambda b, h: (ctxb(b, h), h)),
            pl.BlockSpec((CTX, 2 * HD), lambda b, h: (ctxb(b, h), h)),
            pl.BlockSpec((CTX, 2 * HD), lambda b, h: (ctxb(b, h), h)),
        ],
        out_specs=pl.BlockSpec((CTX, VD), lambda b, h: (b, h)),
        out_shape=jax.ShapeDtypeStruct((N_CTX, W1K), BF16),
        compiler_params=_cparams(("parallel", "parallel")),
        name="attn_b_ctx",
    )(qcat, kcat, vcat)


def _merge_kernel(x_ref, gate_ref, ya_ref, yb_ref, za_ref, zb_ref, cu_ref, cv_ref, cz_ref,
                  ga_ref, gb_ref, gc_ref, lng_ref, lnb_ref, ws_ref, bs_ref,
                  wpa_ref, wpb_ref, wpc_ref, wout_ref, fg_ref, o_ref, *, final):
    tm = x_ref.shape[0]
    nchunk = tm // CHUNK

    def silu(z_ref):
        z = z_ref[...].astype(F32)
        return z * jax.nn.sigmoid(z)

    def branch(y, z_ref, g_ref, w_ref):
        t = (y * silu(z_ref)).astype(BF16)
        return jax.nn.sigmoid(g_ref[...].astype(F32)) * jnp.dot(
            t, w_ref[...], preferred_element_type=F32)

    v = cv_ref[...].astype(F32)
    mu = jnp.mean(v, axis=-1, keepdims=True)
    vc = v - mu
    var = jnp.mean(vc * vc, axis=-1, keepdims=True)
    vn = (vc * lax.rsqrt(var + EPS) * lng_ref[...] + lnb_ref[...]).astype(BF16)
    cols = []
    for g in range(C_GROUPS):
        gs = slice(g * CHUNK, (g + 1) * CHUNK)
        rhs = jnp.concatenate([vn[c * CHUNK:(c + 1) * CHUNK, gs] for c in range(nchunk)], axis=1)
        mx = jnp.dot(ws_ref[g], rhs, preferred_element_type=F32)
        cols.append(jnp.concatenate(
            [mx[:, c * CHUNK:(c + 1) * CHUNK] for c in range(nchunk)], axis=0))
    bias = jnp.concatenate([bs_ref[...]] * nchunk, axis=0)
    yc = cu_ref[...].astype(F32) * (jnp.concatenate(cols, axis=1) + bias)

    m = (branch(ya_ref[...].astype(F32), za_ref, ga_ref, wpa_ref)
         + branch(yb_ref[...].astype(F32), zb_ref, gb_ref, wpb_ref)
         + branch(yc, cz_ref, gc_ref, wpc_ref))
    out = jnp.dot(m.astype(BF16), wout_ref[...], preferred_element_type=F32)
    xn = x_ref[...] + gate_ref[...] * out
    if final:
        ms = jnp.mean(xn * xn, axis=-1, keepdims=True)
        xn = xn * lax.rsqrt(ms + EPS) * fg_ref[...]
    o_ref[...] = xn


def _merge(x_rows, gate, y_a, y_b, proj, lng, lnb, ws, bs, wpa, wpb, wpc, wout, fg, final, is_ctx):
    tm = TM_MERGE
    rows = x_rows.shape[0]
    off = N_LAT // tm if is_ctx else 0
    seg = lambda c: pl.BlockSpec((tm, W1K), lambda i: (i + off, c // W1K))
    seg2 = lambda c: pl.BlockSpec((tm, D), lambda i: (i + off, c // D))
    own = pl.BlockSpec((tm, W1K), lambda i: (i, 0))
    const = lambda shape: pl.BlockSpec(shape, lambda i: (0,) * len(shape),
                                       pipeline_mode=pl.Buffered(1))
    return pl.pallas_call(
        functools.partial(_merge_kernel, final=final),
        grid=(rows // tm,),
        in_specs=[
            pl.BlockSpec((tm, D), lambda i: (i, 0)),
            pl.BlockSpec((None, 1, D), lambda i: ((i + off) // (SEQ // tm), 0, 0)),
            own, own,
            seg(AZ), seg(BZ), seg(CU), seg(CV), seg(CZ),
            seg2(GA), seg2(GB), seg2(GC),
            const((1, W1K)), const((1, W1K)),
            const((C_GROUPS, CHUNK, CHUNK)), const((CHUNK, W1K)),
            const((W1K, D)), const((W1K, D)), const((W1K, D)), const((D, D)),
            const((1, D)),
        ],
        out_specs=pl.BlockSpec((tm, D), lambda i: (i, 0)),
        out_shape=jax.ShapeDtypeStruct((rows, D), F32),
        compiler_params=_cparams(("parallel",)),
        name="merge_final" if final else ("merge_ctx" if is_ctx else "merge"),
    )(x_rows, gate, y_a, y_b, proj, proj, proj, proj, proj, proj, proj, proj,
      lng, lnb, ws, bs, wpa, wpb, wpc, wout, fg)


def _rope_tables(width):
    t = np.arange(SEQ)
    row = (t // GRID_W).astype(np.float64)
    col = (t % GRID_W).astype(np.float64)
    half = width // 4
    lane = np.arange(128)
    u = lane % (width // 2)
    inv = THETA ** (-(u % half).astype(np.float64) / half)
    pos = np.where((lane // (width // 2))[None, :] == 0, row[:, None], col[:, None])
    ang = pos * inv[None, :]
    active = (lane < width)[None, :]
    first = (u < half)[None, :]
    cos = np.where(active, np.cos(ang), 1.0)
    sin = np.where(active, np.sin(ang), 0.0)
    tab = np.stack([cos, np.where(first, -sin, 0.0), np.where(first, 0.0, sin)])
    ident = np.stack([np.ones((TM, 128)), np.zeros((TM, 128)), np.zeros((TM, 128))])
    return jnp.asarray(np.concatenate([tab, ident], axis=1), dtype=F32)


def _split_w_in(w):
    widths = (1024, 256, 256, 1024, 512, 512, 64, 1024, 1024, 1024, 1024, 2048, 2048, 2048)
    offs = [0]
    for wd in widths:
        offs.append(offs[-1] + wd)
    (aq, ak, av, az, bcq, bckv, bkr, bz, cu, cv, cz, ga, gb, gc) = [
        w[:, offs[i]:offs[i + 1]] for i in range(len(widths))]
    main = jnp.concatenate([aq, az, bz, cu, cv, cz, ga, gb, gc, ak, av, bcq, bckv],
                           axis=1).astype(BF16)
    kr = jnp.pad(bkr, ((0, 0), (0, 128 - ROPE))).astype(BF16)
    return main, kr


def kernel(x, c, ctx, c_ctx, ada_w, ada_b, norm_g, w_in, sink_a, mla_gq, mla_gkv, w_uq, w_ukv,
           sgu_ln_g, sgu_ln_b, sgu_w, sgu_b, w_pa, w_pb, w_pc, w_out, final_g):
    assert x.shape == (B, SEQ, D) and ctx.shape == (B, CTX, D)
    x_lat = x.reshape(N_LAT, D)
    x_ctx = ctx.reshape(N_CTX, D)
    cc =jnp.concatenate([c, c_ctx[None, :], jnp.zeros((3, D), F32)], axis=0)
    mod = _adaln(cc, ada_w, ada_b)
    t128 = _rope_tables(HD)
    t64 = _rope_tables(ROPE)

    for l in range(DEPTH):
        shift = mod[l, :, 0:D].reshape(8, 1, D)
        scale = mod[l, :, D:2 * D].reshape(8, 1, D)
        gate = mod[l, :, 2 * D:].reshape(8, 1, D)
        w_main, w_kr = _split_w_in(w_in[l])
        wq = jnp.pad(w_uq[l].reshape(LORA, B_HEADS, NOPE + ROPE),
                     ((0, 0), (0, 0), (0, 2 * HD - NOPE - ROPE))).reshape(LORA, 2 * W1K).astype(BF16)
        wkv = w_ukv[l].reshape(LORA, B_HEADS, 2, HD).transpose(0, 2, 1, 3).reshape(
            LORA, 2 * W1K).astype(BF16)
        bs = jnp.repeat(sgu_b[l].T, CHUNK, axis=1)

        final = l == DEPTH - 1
        proj, kr = _inproj(x_lat, x_ctx, norm_g[l].reshape(1, D), scale, shift, w_main, w_kr,
                           t128, t64, final)
        qcat, kcat, vcat = _mla_proj(proj, kr, mla_gq[l].reshape(1, LORA),
                                     mla_gkv[l].reshape(1, LORA), wq, wkv, t64)
        merge_w = (sgu_ln_g[l].reshape(1, W1K), sgu_ln_b[l].reshape(1, W1K),
                   sgu_w[l].astype(BF16), bs,
                   w_pa[l].astype(BF16), w_pb[l].astype(BF16), w_pc[l].astype(BF16),
                   w_out[l].astype(BF16), final_g.reshape(1, D))
        if not final:
            x_ctx = _merge(x_ctx, gate, _attn_a_ctx(sink_a[l], proj),
                           _attn_b_ctx(qcat, kcat, vcat), proj, *merge_w, False, True)
        x_lat = _merge(x_lat, gate, _attn_a_lat(sink_a[l], proj), _attn_b_lat(qcat, kcat, vcat),
                       proj, *merge_w, final, False)
    return x_lat.reshape(B, SEQ, D)
```

```python
import functools

import jax
import jax.numpy as jnp
import numpy as np
from jax import lax
from jax.experimental import pallas as pl
from jax.experimental.pallas import tpu as pltpu

F32 = jnp.float32
BF16 = jnp.bfloat16

D = 2048
B = 4
SEQ = 4096
DEPTH = 2
GRID_W = 64
CTX = 256
HD = 128
THETA = 10000.0
EPS = 1e-6
A_HEADS = 8
A_KV = 2
A_GROUP = A_HEADS // A_KV
WBLK = 128
B_HEADS = 8
LORA = 512
NOPE = 128
ROPE = 64
VD = 128
C_GROUPS = 8
CHUNK = 128
W1K = 1024
LOG2E = 1.4426950408889634

N_LAT = B * SEQ
N_CTX = B * CTX

AQ, AZ, BZ, CU, CV, CZ = 0, 1024, 2048, 3072, 4096, 5120
GA, GB, GC = 6144, 8192, 10240
AK, AV, BCQ, BCKV = 12288, 12544, 12800, 13312
NPROJ = 13824

TM = 1024
TN = 1536
QB_A = 512
TQ_B = 1024
TM_MERGE = 256

V7X_VMEM_BYTES = 64 * 1024 * 1024
VMEM_LIMIT = V7X_VMEM_BYTES - 8 * 1024 * 1024


def _cparams(sem):
    return pltpu.CompilerParams(dimension_semantics=sem, vmem_limit_bytes=VMEM_LIMIT)


def _rope(x, tab_ref, shift):
    return (x * tab_ref[0]
            + pltpu.roll(x, 128 - shift, 1) * tab_ref[1]
            + pltpu.roll(x, shift, 1) * tab_ref[2])


def _nt_dot(a, b):
    return lax.dot_general(a, b, (((1,), (1,)), ((), ())), preferred_element_type=F32)


def _adaln_kernel(c_ref, w_ref, b_ref, o_ref):
    c = c_ref[...]
    a = (c * jax.nn.sigmoid(c)).astype(BF16)
    o_ref[...] = jnp.dot(a, w_ref[...].astype(BF16), preferred_element_type=F32) + b_ref[...]


def _adaln(cc, ada_w, ada_b):
    tn = 768
    return pl.pallas_call(
        _adaln_kernel,
        grid=(DEPTH, 3 * D // tn),
        in_specs=[
            pl.BlockSpec((8, D), lambda l, j: (0, 0)),
            pl.BlockSpec((None, D, tn), lambda l, j: (l, 0, j)),
            pl.BlockSpec((None, 1, tn), lambda l, j: (l, 0, j)),
        ],
        out_specs=pl.BlockSpec((None, 8, tn), lambda l, j: (l, 0, j)),
        out_shape=jax.ShapeDtypeStruct((DEPTH, 8, 3 * D), F32),
        compiler_params=_cparams(("parallel", "parallel")),
        name="adaln",
    )(cc, ada_w, ada_b.reshape(DEPTH, 1, 3 * D))


def _inproj_kernel(*refs, rope, j_q, j_k):
    if rope:
        (x_ref, g_ref, sc_ref, sh_ref, w_ref, wkr_ref, t128_ref, t64_ref,
         o_ref, kr_ref, h_ref) = refs
    else:
        x_ref, g_ref, sc_ref, sh_ref, w_ref, wkr_ref, o_ref, kr_ref, h_ref = refs
    j = pl.program_id(1)

    @pl.when(j == 0)
    def _():
        x = x_ref[...]
        ms = jnp.mean(x * x, axis=-1, keepdims=True)
        h = (x * lax.rsqrt(ms + EPS) * g_ref[...]) * (1.0 + sc_ref[...]) + sh_ref[...]
        hb = h.astype(BF16)
        h_ref[...] = hb
        kr = jnp.dot(hb, wkr_ref[...], preferred_element_type=F32)
        kr_ref[...] = _rope(kr, t64_ref, ROPE // 4) if rope else kr

    def mm():
        return jnp.dot(h_ref[...], w_ref[...], preferred_element_type=F32)

    def store_heads(acc, n_heads, scale):
        for hh in range(n_heads):
            sl = slice(hh * HD, (hh + 1) * HD)
            blk = _rope(acc[:, sl], t128_ref, HD // 4) if rope else acc[:, sl]
            o_ref[:, sl] = (blk * scale).astype(BF16)
        o_ref[:, n_heads * HD:] = acc[:, n_heads * HD:].astype(BF16)

    special = []
    if j_q is not None:
        special.append(j_q)

        @pl.when(j == j_q)
        def _():
            store_heads(mm(), A_HEADS, HD ** -0.5 * LOG2E)

    if rope:
        special.append(j_k)

        @pl.when(j == j_k)
        def _():
            store_heads(mm(), A_KV, 1.0)

    def plain():
        o_ref[...] = mm().astype(BF16)

    if special:
        pl.when(functools.reduce(jnp.logical_and, [j != s for s in special]))(plain)
    else:
        plain()


def _inproj(x_rows, norm_g, scale, shift, group0, w_main, w_kr, tabs, col0):
    rows = x_rows.shape[0]
    assert rows % TM == 0 and col0 % TN == 0 and AQ % TN == 0 and AK % TN == 0
    rope = tabs is not None
    tab_idx = lambda i, j: (0, i % (SEQ // TM), 0)
    grp = lambda i, j: (group0 + i // (SEQ // TM), 0, 0)
    in_specs = [
        pl.BlockSpec((TM, D), lambda i, j: (i, 0)),
        pl.BlockSpec((1, D), lambda i, j: (0, 0)),
        pl.BlockSpec((None, 1, D), grp),
        pl.BlockSpec((None, 1, D), grp),
        pl.BlockSpec((D, TN), lambda i, j: (0, j + col0 // TN)),
        pl.BlockSpec((D, 128), lambda i, j: (0, 0)),
    ]
    if rope:
        in_specs += [pl.BlockSpec((3, TM, 128), tab_idx), pl.BlockSpec((3, TM, 128), tab_idx)]
    return pl.pallas_call(
        functools.partial(_inproj_kernel, rope=rope,
                          j_q=(AQ - col0) // TN if AQ >= col0 else None, j_k=(AK - col0) // TN),
        grid=(rows // TM, (NPROJ - col0) // TN),
        in_specs=in_specs,
        out_specs=[
            pl.BlockSpec((TM, TN), lambda i, j: (i, j)),
            pl.BlockSpec((TM, 128), lambda i, j: (i, 0)),
        ],
        out_shape=[
            jax.ShapeDtypeStruct((rows, NPROJ - col0), BF16),
            jax.ShapeDtypeStruct((rows, 128), F32),
        ],
        scratch_shapes=[pltpu.VMEM((TM, D), BF16)],
        compiler_params=_cparams(("parallel", "arbitrary")),
        name="inproj" if rope else "inproj_ctx",
    )(x_rows, norm_g, scale, shift, w_main, w_kr, *(tabs if rope else ()))


def _mla_proj_kernel(*refs, rope, need_q):
    refs = list(refs)
    cq_ref = refs.pop(0) if need_q else None
    ckv_ref, kr_ref = refs.pop(0), refs.pop(0)
    gq_ref = refs.pop(0) if need_q else None
    gkv_ref = refs.pop(0)
    wq_ref = refs.pop(0) if need_q else None
    wkv_ref = refs.pop(0)
    t64_ref = refs.pop(0) if rope else None
    q_ref = refs.pop(0) if need_q else None
    k_ref, v_ref = refs

    def nrm(x_ref, g_ref):
        xf = x_ref[...].astype(F32)
        ms = jnp.mean(xf * xf, axis=-1, keepdims=True)
        return (xf * lax.rsqrt(ms + EPS) * g_ref[...]).astype(BF16)

    kv = jnp.dot(nrm(ckv_ref, gkv_ref), wkv_ref[...], preferred_element_type=F32)
    krb = kr_ref[...].astype(BF16)
    ones = jnp.ones((kv.shape[0], HD), BF16)
    for h in range(B_HEADS):
        lo = h * 2 * HD
        k_ref[:, lo:lo + HD] = kv[:, h * HD:(h + 1) * HD].astype(BF16)
        k_ref[:, lo + HD:lo + 2 * HD] = krb
        v_ref[:, lo:lo + HD] = kv[:, (B_HEADS + h) * HD:(B_HEADS + h + 1) * HD].astype(BF16)
        v_ref[:, lo + HD:lo + 2 * HD] = ones
    if need_q:
        qscale = (NOPE + ROPE) ** -0.5 * LOG2E
        q = jnp.dot(nrm(cq_ref, gq_ref), wq_ref[...], preferred_element_type=F32)
        for h in range(B_HEADS):
            lo = h * 2 * HD
            q_ref[:, lo:lo + HD] = (q[:, lo:lo + HD] * qscale).astype(BF16)
            qr = q[:, lo + HD:lo + 2 * HD]
            if rope:
                qr = _rope(qr, t64_ref, ROPE // 4)
            q_ref[:, lo + HD:lo + 2 * HD] = (qr * qscale).astype(BF16)


def _mla_proj(proj, col0, kr, gq, gkv, wq, wkv, t64, need_q):
    rows = proj.shape[0]
    rope = t64 is not None
    row_blk = lambda w: pl.BlockSpec((TM, w), lambda i: (i, 0))
    const = lambda a: pl.BlockSpec(a.shape, lambda i: (0, 0))
    ins, specs = [], []
    if need_q:
        ins.append(proj)
        specs.append(pl.BlockSpec((TM, LORA), lambda i: (i, (BCQ - col0) // LORA)))
    ins += [proj, kr]
    specs += [pl.BlockSpec((TM, LORA), lambda i: (i, (BCKV - col0) // LORA)), row_blk(128)]
    if need_q:
        ins.append(gq)
        specs.append(const(gq))
    ins.append(gkv)
    specs.append(const(gkv))
    if need_q:
        ins.append(wq)
        specs.append(const(wq))
    ins.append(wkv)
    specs.append(const(wkv))
    if rope:
        ins.append(t64)
        specs.append(pl.BlockSpec((3, TM, 128), lambda i: (0, i % (SEQ // TM), 0)))
    n_out = 3 if need_q else 2
    return pl.pallas_call(
        functools.partial(_mla_proj_kernel, rope=rope, need_q=need_q),
        grid=(rows // TM,),
        in_specs=specs,
        out_specs=[row_blk(2 * W1K)] * n_out,
        out_shape=[jax.ShapeDtypeStruct((rows, 2 * W1K), BF16)] * n_out,
        compiler_params=_cparams(("parallel",)),
        name="mla_proj" if rope else "mla_proj_ctx",
    )(*ins)


def _stack_heads(q):
    return jnp.concatenate([q[:, g * HD:(g + 1) * HD] for g in range(A_GROUP)], axis=0)


def _unstack_heads(o, rows):
    return jnp.concatenate([o[g * rows:(g + 1) * rows, :] for g in range(A_GROUP)], axis=1)


def _sink_col(sink_ref, kvh, rows):
    return jnp.concatenate(
        [jnp.full((rows, HD), sink_ref[kvh * A_GROUP + g] * LOG2E, F32) for g in range(A_GROUP)],
        axis=0)


def _with_ones(v):
    return jnp.concatenate([v, jnp.ones(v.shape, v.dtype)], axis=1)


def _attn_a_lat_kernel(sink_ref, q_ref, kp_ref, kc_ref, kn_ref, vp_ref, vc_ref, vn_ref,
                       kx_ref, vx_ref, o_ref):
    kvh = pl.program_id(1)
    qb = pl.program_id(2)
    nsub = QB_A // WBLK
    kfull = jnp.concatenate([kp_ref[...], kc_ref[...], kn_ref[...]], axis=0)
    vfull = _with_ones(jnp.concatenate([vp_ref[...], vc_ref[...], vn_ref[...]], axis=0))
    kx = kx_ref[...]
    vx = _with_ones(vx_ref[...])
    sink = _sink_col(sink_ref, kvh, WBLK)
    rows = A_GROUP * WBLK
    ti = lax.broadcasted_iota(jnp.int32, (rows, 3 * WBLK), 0) % WBLK
    kj = lax.broadcasted_iota(jnp.int32, (rows, 3 * WBLK), 1)
    rel = kj - ti
    band = jnp.logical_and(rel >= 0, rel <= 2 * WBLK)
    first = qb == 0
    last = qb == SEQ // QB_A - 1
    for s in range(nsub):
        qs = _stack_heads(q_ref[s * WBLK:(s + 1) * WBLK, :])
        kw = kfull[s * WBLK:(s + 3) * WBLK, :]
        vw = vfull[s * WBLK:(s + 3) * WBLK, :]
        valid = band
        if s == 0:
            valid = jnp.logical_and(valid, jnp.logical_not(jnp.logical_and(kj < WBLK, first)))
        if s == nsub - 1:
            valid = jnp.logical_and(valid, jnp.logical_not(jnp.logical_and(kj >= 2 * WBLK, last)))
        s_loc = jnp.where(valid, _nt_dot(qs, kw), -1e30)
        s_ctx = _nt_dot(qs, kx)
        m = jnp.maximum(jnp.maximum(jnp.max(s_loc, axis=-1, keepdims=True),
                                    jnp.max(s_ctx, axis=-1, keepdims=True)), sink)
        p_loc = jnp.exp2(s_loc - m[:, :1]).astype(BF16)
        p_ctx = jnp.exp2(s_ctx - m[:, :1]).astype(BF16)
        res = (jnp.dot(p_loc, vw, preferred_element_type=F32)
               + jnp.dot(p_ctx, vx, preferred_element_type=F32))
        o = res[:, :HD] / (res[:, HD:] + jnp.exp2(sink - m))
        o_ref[s * WBLK:(s + 1) * WBLK, :] = _unstack_heads(o, WBLK).astype(BF16)


def _attn_a_lat(sink, proj, proj_ctx, col0):
    nqb = SEQ // QB_A
    sub = QB_A // WBLK
    nblk = SEQ // WBLK
    row = lambda b, k, q: b * nqb + q
    prev = lambda b, k, q: b * nblk + jnp.maximum(q * sub - 1, 0)
    nxt = lambda b, k, q: b * nblk + jnp.minimum(q * sub + sub, nblk - 1)
    return pl.pallas_call(
        _attn_a_lat_kernel,
        grid=(B, A_KV, nqb),
        in_specs=[
            pl.BlockSpec(memory_space=pltpu.SMEM),
            pl.BlockSpec((QB_A, A_GROUP * HD), lambda b, k, q: (row(b, k, q), AQ // (A_GROUP * HD) + k)),
            pl.BlockSpec((WBLK, HD), lambda b, k, q: (prev(b, k, q), AK // HD + k)),
            pl.BlockSpec((QB_A, HD), lambda b, k, q: (row(b, k, q), AK // HD + k)),
            pl.BlockSpec((WBLK, HD), lambda b, k, q: (nxt(b, k, q), AK // HD + k)),
            pl.BlockSpec((WBLK, HD), lambda b, k, q: (prev(b, k, q), AV // HD + k)),
            pl.BlockSpec((QB_A, HD), lambda b, k, q: (row(b, k, q), AV // HD + k)),
            pl.BlockSpec((WBLK, HD), lambda b, k, q: (nxt(b, k, q), AV // HD + k)),
            pl.BlockSpec((CTX, HD), lambda b, k, q: (b, (AK - col0) // HD + k)),
            pl.BlockSpec((CTX, HD), lambda b, k, q: (b, (AV - col0) // HD + k)),
        ],
        out_specs=pl.BlockSpec((QB_A, A_GROUP * HD), lambda b, k, q: (row(b, k, q), k)),
        out_shape=jax.ShapeDtypeStruct((N_LAT, W1K), BF16),
        compiler_params=_cparams(("parallel", "parallel", "parallel")),
        name="attn_a_lat",
    )(sink, proj, proj, proj, proj, proj, proj, proj, proj_ctx, proj_ctx)


def _attn_a_ctx_kernel(sink_ref, q_ref, kx_ref, vx_ref, o_ref):
    kvh = pl.program_id(1)
    qs = _stack_heads(q_ref[...])
    sink = _sink_col(sink_ref, kvh, CTX)
    s_ctx = _nt_dot(qs, kx_ref[...])
    m = jnp.maximum(jnp.max(s_ctx, axis=-1, keepdims=True), sink)
    p = jnp.exp2(s_ctx - m[:, :1]).astype(BF16)
    res = jnp.dot(p, _with_ones(vx_ref[...]), preferred_element_type=F32)
    o = res[:, :HD] / (res[:, HD:] + jnp.exp2(sink - m))
    o_ref[...] = _unstack_heads(o, CTX).astype(BF16)


def _attn_a_ctx(sink, proj_ctx):
    return pl.pallas_call(
        _attn_a_ctx_kernel,
        grid=(B, A_KV),
        in_specs=[
            pl.BlockSpec(memory_space=pltpu.SMEM),
            pl.BlockSpec((CTX, A_GROUP * HD), lambda b, k: (b, AQ // (A_GROUP * HD) + k)),
            pl.BlockSpec((CTX, HD), lambda b, k: (b, AK // HD + k)),
            pl.BlockSpec((CTX, HD), lambda b, k: (b, AV // HD + k)),
        ],
        out_specs=pl.BlockSpec((CTX, A_GROUP * HD), lambda b, k: (b, k)),
        out_shape=jax.ShapeDtypeStruct((N_CTX, W1K), BF16),
        compiler_params=_cparams(("parallel", "parallel")),
        name="attn_a_ctx",
    )(sink, proj_ctx, proj_ctx, proj_ctx)


def _attn_b_lat_kernel(q_ref, kl_ref, kx_ref, vl_ref, vx_ref, o_ref,
                       sa_ref, ma_ref, sb_ref, mb_ref):
    g = pl.program_id(0)

    @pl.when(g == 0)
    def _():
        sb_ref[...] = jnp.zeros(sb_ref.shape, F32)
        mb_ref[...] = jnp.zeros(mb_ref.shape, F32)

    def step(s_cur, m_cur, s_prev, m_prev):
        q = q_ref[...]
        s1 = _nt_dot(q, kl_ref[...])
        s2 = _nt_dot(q, kx_ref[...])
        s_cur[:, :SEQ] = s1
        s_cur[:, SEQ:] = s2
        m_cur[...] = jnp.maximum(jnp.max(s1, axis=-1, keepdims=True),
                                 jnp.max(s2, axis=-1, keepdims=True))
        m = m_prev[...]
        p1 = jnp.exp2(s_prev[:, :SEQ] - m).astype(BF16)
        p2 = jnp.exp2(s_prev[:, SEQ:] - m).astype(BF16)
        res = (jnp.dot(p1, vl_ref[...], preferred_element_type=F32)
               + jnp.dot(p2, vx_ref[...], preferred_element_type=F32))
        o_ref[...] = (res[:, :VD] / res[:, VD:]).astype(BF16)

    @pl.when(g % 2 == 0)
    def _():
        step(sa_ref, ma_ref, sb_ref, mb_ref)

    @pl.when(g % 2 == 1)
    def _():
        step(sb_ref, mb_ref, sa_ref, ma_ref)


def _attn_b_lat(qcat, kcat, vcat, kcat_ctx, vcat_ctx):
    nq = SEQ // TQ_B
    per_b = B_HEADS * nq
    n = B * per_b
    cur = lambda g: jnp.minimum(g, n - 1)
    prv = lambda g: jnp.maximum(g - 1, 0)
    bat = lambda t: t // per_b
    head = lambda t: (t // nq) % B_HEADS
    qrow = lambda t: bat(t) * nq + t % nq
    return pl.pallas_call(
        _attn_b_lat_kernel,
        grid=(n + 1,),
        in_specs=[
            pl.BlockSpec((TQ_B, 2 * HD), lambda g: (qrow(cur(g)), head(cur(g)))),
            pl.BlockSpec((SEQ, 2 * HD), lambda g: (bat(cur(g)), head(cur(g)))),
            pl.BlockSpec((CTX, 2 * HD), lambda g: (bat(cur(g)), head(cur(g)))),
            pl.BlockSpec((SEQ, 2 * HD), lambda g: (bat(prv(g)), head(prv(g)))),
            pl.BlockSpec((CTX, 2 * HD), lambda g: (bat(prv(g)), head(prv(g)))),
        ],
        out_specs=pl.BlockSpec((TQ_B, VD), lambda g: (qrow(prv(g)), head(prv(g)))),
        out_shape=jax.ShapeDtypeStruct((N_LAT, W1K), BF16),
        scratch_shapes=[pltpu.VMEM((TQ_B, SEQ + CTX), F32), pltpu.VMEM((TQ_B, 1), F32),
                        pltpu.VMEM((TQ_B, SEQ + CTX), F32), pltpu.VMEM((TQ_B, 1), F32)],
        compiler_params=_cparams(("arbitrary",)),
        name="attn_b_lat",
    )(qcat, kcat, kcat_ctx, vcat, vcat_ctx)


def _attn_b_ctx_kernel(q_ref, kx_ref, vx_ref, o_ref):
    s = _nt_dot(q_ref[...], kx_ref[...])
    p = jnp.exp2(s - jnp.max(s, axis=-1, keepdims=True)).astype(BF16)
    res = jnp.dot(p, vx_ref[...], preferred_element_type=F32)
    o_ref[...] = (res[:, :VD] / res[:, VD:]).astype(BF16)


def _attn_b_ctx(qcat, kcat, vcat):
    blk = pl.BlockSpec((CTX, 2 * HD), lambda b, h: (b, h))
    return pl.pallas_call(
        _attn_b_ctx_kernel,
        grid=(B, B_HEADS),
        in_specs=[blk, blk, blk],
        out_specs=pl.BlockSpec((CTX, VD), lambda b, h: (b, h)),
        out_shape=jax.ShapeDtypeStruct((N_CTX, W1K), BF16),
        compiler_params=_cparams(("parallel", "parallel")),
        name="attn_b_ctx",
    )(qcat, kcat, vcat)


def _merge_kernel(x_ref, gate_ref, ya_ref, yb_ref, za_ref, zb_ref, cu_ref, cv_ref, cz_ref,
                  ga_ref, gb_ref, gc_ref, lng_ref, lnb_ref, ws_ref, bs_ref,
                  wpa_ref, wpb_ref, wpc_ref, wout_ref, fg_ref, o_ref, *, final):
    tm = x_ref.shape[0]
    nchunk = tm // CHUNK

    def silu(z_ref):
        z = z_ref[...].astype(F32)
        return z * jax.nn.sigmoid(z)

    def branch(y, z_ref, g_ref, w_ref):
        t = (y * silu(z_ref)).astype(BF16)
        return jax.nn.sigmoid(g_ref[...].astype(F32)) * jnp.dot(
            t, w_ref[...], preferred_element_type=F32)

    v = cv_ref[...].astype(F32)
    mu = jnp.mean(v, axis=-1, keepdims=True)
    vc = v - mu
    var = jnp.mean(vc * vc, axis=-1, keepdims=True)
    vn = (vc * lax.rsqrt(var + EPS) * lng_ref[...] + lnb_ref[...]).astype(BF16)
    cols = []
    for g in range(C_GROUPS):
        gs = slice(g * CHUNK, (g + 1) * CHUNK)
        rhs = jnp.concatenate([vn[c * CHUNK:(c + 1) * CHUNK, gs] for c in range(nchunk)], axis=1)
        mx = jnp.dot(ws_ref[g], rhs, preferred_element_type=F32)
        cols.append(jnp.concatenate(
            [mx[:, c * CHUNK:(c + 1) * CHUNK] for c in range(nchunk)], axis=0))
    bias = jnp.concatenate([bs_ref[...]] * nchunk, axis=0)
    yc = cu_ref[...].astype(F32) * (jnp.concatenate(cols, axis=1) + bias)

    m = (branch(ya_ref[...].astype(F32), za_ref, ga_ref, wpa_ref)
         + branch(yb_ref[...].astype(F32), zb_ref, gb_ref, wpb_ref)
         + branch(yc, cz_ref, gc_ref, wpc_ref))
    out = jnp.dot(m.astype(BF16), wout_ref[...], preferred_element_type=F32)
    xn = x_ref[...] + gate_ref[...] * out
    if final:
        ms = jnp.mean(xn * xn, axis=-1, keepdims=True)
        xn = xn * lax.rsqrt(ms + EPS) * fg_ref[...]
    o_ref[...] = xn


def _merge(x_rows, gate, group0, y_a, y_b, proj, lng, lnb, ws, bs, wpa, wpb, wpc, wout, fg, final):
    tm = TM_MERGE
    seg = lambda c: pl.BlockSpec((tm, W1K), lambda i: (i, c // W1K))
    seg2 = lambda c: pl.BlockSpec((tm, D), lambda i: (i, c // D))
    const = lambda shape: pl.BlockSpec(shape, lambda i: (0,) * len(shape),
                                       pipeline_mode=pl.Buffered(1))
    return pl.pallas_call(
        functools.partial(_merge_kernel, final=final),
        grid=(x_rows.shape[0] // tm,),
        in_specs=[
            pl.BlockSpec((tm, D), lambda i: (i, 0)),
            pl.BlockSpec((None, 1, D), lambda i: (group0 + i // (SEQ // tm), 0, 0)),
            seg(0), seg(0),
            seg(AZ), seg(BZ), seg(CU), seg(CV), seg(CZ),
            seg2(GA), seg2(GB), seg2(GC),
            const((1, W1K)), const((1, W1K)),
            const((C_GROUPS, CHUNK, CHUNK)), const((CHUNK, W1K)),
            const((W1K, D)), const((W1K, D)), const((W1K, D)), const((D, D)),
            const((1, D)),
        ],
        out_specs=pl.BlockSpec((tm, D), lambda i: (i, 0)),
        out_shape=jax.ShapeDtypeStruct(x_rows.shape, F32),
        compiler_params=_cparams(("parallel",)),
        name="merge_final" if final else "merge",
    )(x_rows, gate, y_a, y_b, proj, proj, proj, proj, proj, proj, proj, proj,
      lng, lnb, ws, bs, wpa, wpb, wpc, wout, fg)


def _rope_tables(width):
    t = np.arange(SEQ)
    row = (t // GRID_W).astype(np.float64)
    col = (t % GRID_W).astype(np.float64)
    half = width // 4
    lane = np.arange(128)
    u = lane % (width // 2)
    inv = THETA ** (-(u % half).astype(np.float64) / half)
    pos = np.where((lane // (width // 2))[None, :] == 0, row[:, None], col[:, None])
    ang = pos * inv[None, :]
    active = (lane < width)[None, :]
    first = (u < half)[None, :]
    cos = np.where(active, np.cos(ang), 1.0)
    sin = np.where(active, np.sin(ang), 0.0)
    tab = np.stack([cos, np.where(first, -sin, 0.0), np.where(first, 0.0, sin)])
    return jnp.asarray(tab, dtype=F32)


PIECE = 512
N_ALIGNED = 7
N_PIECES = 27
KR_COL = N_ALIGNED * PIECE


def _prep_w_kernel(a_ref, b_ref, tail_ref, o_ref):
    p = pl.program_id(2)

    @pl.when(p < N_ALIGNED)
    def _():
        o_ref[...] = a_ref[...].astype(BF16)

    @pl.when(p >= N_ALIGNED)
    def _():
        lane = lax.broadcasted_iota(jnp.int32, (a_ref.shape[0], 128), 1)
        nt = PIECE // 128
        rolled = [pltpu.roll(a_ref[:, t * 128:(t + 1) * 128], 64, 1) for t in range(nt)]
        nxt = jnp.where(p == N_PIECES - 1, tail_ref[...], b_ref[...])
        rolled.append(pltpu.roll(nxt, 64, 1))
        for t in range(nt):
            o_ref[:, t * 128:(t + 1) * 128] = jnp.where(
                lane < 64, rolled[t], rolled[t + 1]).astype(BF16)


def _prep_w_in(w_in):
    tr = 1024
    n_cols = w_in.shape[-1]
    last_full = n_cols // 128 - 1
    tail = jnp.pad(w_in[:, :, (last_full + 1) * 128:],
                   ((0, 0), (0, 0), (0, (last_full + 2) * 128 - n_cols)))

    def dst(p):
        d = jnp.where(p >= 3, p - 1, p)
        d = jnp.where(p >= 5, p + 20, d)
        d = jnp.where(p == 2, AK // PIECE, d)
        return jnp.where(p >= N_ALIGNED, p - 3, d)

    return pl.pallas_call(
        _prep_w_kernel,
        grid=(DEPTH, D // tr, N_PIECES),
        in_specs=[
            pl.BlockSpec((None, tr, PIECE), lambda l, r, p: (l, r, p)),
            pl.BlockSpec((None, tr, 128),
                         lambda l, r, p: (l, r, jnp.minimum((p + 1) * (PIECE // 128), last_full))),
            pl.BlockSpec((None, tr, 128), lambda l, r, p: (l, r, 0)),
        ],
        out_specs=pl.BlockSpec((None, tr, PIECE), lambda l, r, p: (l, r, dst(p))),
        out_shape=jax.ShapeDtypeStruct((DEPTH, D, NPROJ), BF16),
        compiler_params=_cparams(("parallel", "parallel", "parallel")),
        name="prep_w_in",
    )(w_in, w_in, tail)


def kernel(x, c, ctx, c_ctx, ada_w, ada_b, norm_g, w_in, sink_a, mla_gq, mla_gkv, w_uq, w_ukv,
           sgu_ln_g, sgu_ln_b, sgu_w, sgu_b, w_pa, w_pb, w_pc, w_out, final_g):
    assert x.shape == (B, SEQ, D) and ctx.shape == (B, CTX, D)
    x_lat = x.reshape(N_LAT, D)
    x_ctx = ctx.reshape(N_CTX, D)
    cc = jnp.concatenate([c, c_ctx[None, :], jnp.zeros((8 - B - 1, D), F32)], axis=0)
    mod = _adaln(cc, ada_w, ada_b)
    t128 = _rope_tables(HD)
    t64 = _rope_tables(ROPE)
    w_all = _prep_w_in(w_in)

    for l in range(DEPTH):
        final = l == DEPTH - 1
        shift = mod[l, :, 0:D].reshape(8, 1, D)
        scale = mod[l, :, D:2 * D].reshape(8, 1, D)
        gate = mod[l, :, 2 * D:].reshape(8, 1, D)
        w_main = w_all[l]
        w_kr = jnp.pad(w_in[l][:, KR_COL:KR_COL + ROPE], ((0, 0), (0, 128 - ROPE))).astype(BF16)
        wq = jnp.pad(w_uq[l].reshape(LORA, B_HEADS, NOPE + ROPE),
                     ((0, 0), (0, 0), (0, 2 * HD - NOPE - ROPE))).reshape(LORA, 2 * W1K).astype(BF16)
        wkv = w_ukv[l].reshape(LORA, B_HEADS, 2, HD).transpose(0, 2, 1, 3).reshape(
            LORA, 2 * W1K).astype(BF16)
        bs = jnp.repeat(sgu_b[l].T, CHUNK, axis=1)
        ng = norm_g[l].reshape(1, D)
        gq = mla_gq[l].reshape(1, LORA)
        gkv = mla_gkv[l].reshape(1, LORA)
        merge_w = (sgu_ln_g[l].reshape(1, W1K), sgu_ln_b[l].reshape(1, W1K),
                   sgu_w[l].astype(BF16), bs,
                   w_pa[l].astype(BF16), w_pb[l].astype(BF16), w_pc[l].astype(BF16),
                   w_out[l].astype(BF16), final_g.reshape(1, D))

        col0 = AK if final else 0
        proj_c, kr_c = _inproj(x_ctx, ng, scale, shift, B, w_main, w_kr, None, col0)
        ctx_kv = _mla_proj(proj_c, col0, kr_c, gq, gkv, wq, wkv, None, not final)
        kcat_c, vcat_c = ctx_kv[-2:]

        proj, kr = _inproj(x_lat, ng, scale, shift, 0, w_main, w_kr, (t128, t64), 0)
        qcat, kcat, vcat = _mla_proj(proj, 0, kr, gq, gkv, wq, wkv, t64, True)
        y_a = _attn_a_lat(sink_a[l], proj, proj_c, col0)
        y_b = _attn_b_lat(qcat, kcat, vcat, kcat_c, vcat_c)
        if not final:
            x_ctx = _merge(x_ctx, gate, B, _attn_a_ctx(sink_a[l], proj_c),
                           _attn_b_ctx(ctx_kv[0], kcat_c, vcat_c), proj_c, *merge_w, False)
        x_lat = _merge(x_lat, gate, 0, y_a, y_b, proj, *merge_w, final)
    return x_lat.reshape(B, SEQ, D)
```

```python
import functools

import jax
import jax.numpy as jnp
import numpy as np
from jax import lax
from jax.experimental import pallas as pl
from jax.experimental.pallas import tpu as pltpu

F32 = jnp.float32
BF16 = jnp.bfloat16

D = 2048
B = 4
SEQ = 4096
DEPTH = 2
GRID_W = 64
CTX = 256
HD = 128
THETA = 10000.0
EPS = 1e-6
A_HEADS = 8
A_KV = 2
A_GROUP = A_HEADS // A_KV
WBLK = 128
B_HEADS = 8
LORA = 512
NOPE = 128
ROPE = 64
VD = 128
C_GROUPS = 8
CHUNK = 128
W1K = 1024
LOG2E = 1.4426950408889634

N_LAT = B * SEQ
N_CTX = B * CTX

AQ, AZ, BZ, CU, CV, CZ = 0, 1024, 2048, 3072, 4096, 5120
GA, GB, GC = 6144, 8192, 10240
AK, AV, BCQ, BCKV = 12288, 12544, 12800, 13312
NPROJ = 13824

TM = 1024
TN = 1536
QB_A = 512
TQ_B = 1024
TM_MERGE = 256

V7X_VMEM_BYTES = 64 * 1024 * 1024
VMEM_LIMIT = V7X_VMEM_BYTES - 8 * 1024 * 1024


def _cparams(sem):
    return pltpu.CompilerParams(dimension_semantics=sem, vmem_limit_bytes=VMEM_LIMIT)


def _rope(x, tab_ref, shift):
    return (x * tab_ref[0]
            + pltpu.roll(x, 128 - shift, 1) * tab_ref[1]
            + pltpu.roll(x, shift, 1) * tab_ref[2])


def _nt_dot(a, b):
    return lax.dot_general(a, b, (((1,), (1,)), ((), ())), preferred_element_type=F32)


def _adaln_kernel(c_ref, w_ref, b_ref, o_ref):
    c = c_ref[...]
    a = (c * jax.nn.sigmoid(c)).astype(BF16)
    o_ref[...] = jnp.dot(a, w_ref[...].astype(BF16), preferred_element_type=F32) + b_ref[...]


def _adaln(cc, ada_w, ada_b):
    tn = 768
    return pl.pallas_call(
        _adaln_kernel,
        grid=(DEPTH, 3 * D // tn),
        in_specs=[
            pl.BlockSpec((8, D), lambda l, j: (0, 0)),
            pl.BlockSpec((None, D, tn), lambda l, j: (l, 0, j)),
            pl.BlockSpec((None, 1, tn), lambda l, j: (l, 0, j)),
        ],
        out_specs=pl.BlockSpec((None, 8, tn), lambda l, j: (l, 0, j)),
        out_shape=jax.ShapeDtypeStruct((DEPTH, 8, 3 * D), F32),
        compiler_params=_cparams(("parallel", "parallel")),
        name="adaln",
    )(cc, ada_w, ada_b.reshape(DEPTH, 1, 3 * D))


def _inproj_kernel(*refs, rope, j_q, j_k):
    if rope:
        (x_ref, g_ref, sc_ref, sh_ref, w_ref, wkr_ref, t128_ref, t64_ref,
         o_ref, kr_ref, h_ref) = refs
    else:
        x_ref, g_ref, sc_ref, sh_ref, w_ref, wkr_ref, o_ref, kr_ref, h_ref = refs
    j = pl.program_id(1)

    @pl.when(j == 0)
    def _():
        x = x_ref[...]
        ms = jnp.mean(x * x, axis=-1, keepdims=True)
        h = (x * lax.rsqrt(ms + EPS) * g_ref[...]) * (1.0 + sc_ref[...]) + sh_ref[...]
        hb = h.astype(BF16)
        h_ref[...] = hb
        kr = _nt_dot(hb, wkr_ref[...].astype(BF16))
        kr = jnp.concatenate([kr, jnp.zeros((kr.shape[0], 128 - ROPE), F32)], axis=1)
        kr_ref[...] = _rope(kr, t64_ref, ROPE // 4) if rope else kr

    def mm():
        return _nt_dot(h_ref[...], w_ref[...])

    def store_heads(acc, n_heads, scale):
        for hh in range(n_heads):
            sl = slice(hh * HD, (hh + 1) * HD)
            blk = _rope(acc[:, sl], t128_ref, HD // 4) if rope else acc[:, sl]
            o_ref[:, sl] = (blk * scale).astype(BF16)
        o_ref[:, n_heads * HD:] = acc[:, n_heads * HD:].astype(BF16)

    special = []
    if j_q is not None:
        special.append(j_q)

        @pl.when(j == j_q)
        def _():
            store_heads(mm(), A_HEADS, HD ** -0.5 * LOG2E)

    if rope:
        special.append(j_k)

        @pl.when(j == j_k)
        def _():
            store_heads(mm(), A_KV, 1.0)

    def plain():
        o_ref[...] = mm().astype(BF16)

    if special:
        pl.when(functools.reduce(jnp.logical_and, [j != s for s in special]))(plain)
    else:
        plain()


def _inproj(x_rows, norm_g, scale, shift, group0, w_t, w_in_t, layer, tabs, col0):
    rows = x_rows.shape[0]
    assert rows % TM == 0 and col0 % TN == 0 and AQ % TN == 0 and AK % TN == 0
    rope = tabs is not None
    tab_idx = lambda i, j: (0, i % (SEQ // TM), 0)
    grp = lambda i, j: (group0 + i // (SEQ // TM), 0, 0)
    in_specs = [
        pl.BlockSpec((TM, D), lambda i, j: (i, 0)),
        pl.BlockSpec((1, D), lambda i, j: (0, 0)),
        pl.BlockSpec((None, 1, D), grp),
        pl.BlockSpec((None, 1, D), grp),
        pl.BlockSpec((None, TN, D), lambda i, j: (layer, j + col0 // TN, 0)),
        pl.BlockSpec((None, ROPE, D), lambda i, j: (layer, KR_COL // ROPE, 0)),
    ]
    if rope:
        in_specs += [pl.BlockSpec((3, TM, 128), tab_idx), pl.BlockSpec((3, TM, 128), tab_idx)]
    return pl.pallas_call(
        functools.partial(_inproj_kernel, rope=rope,
                          j_q=(AQ - col0) // TN if AQ >= col0 else None, j_k=(AK - col0) // TN),
        grid=(rows // TM, (NPROJ - col0) // TN),
        in_specs=in_specs,
        out_specs=[
            pl.BlockSpec((TM, TN), lambda i, j: (i, j)),
            pl.BlockSpec((TM, 128), lambda i, j: (i, 0)),
        ],
        out_shape=[
            jax.ShapeDtypeStruct((rows, NPROJ - col0), BF16),
            jax.ShapeDtypeStruct((rows, 128), F32),
        ],
        scratch_shapes=[pltpu.VMEM((TM, D), BF16)],
        compiler_params=_cparams(("parallel", "arbitrary")),
        name="inproj" if rope else "inproj_ctx",
    )(x_rows, norm_g, scale, shift, w_t, w_in_t, *(tabs if rope else ()))


def _mla_proj_kernel(*refs, rope, need_q):
    refs = list(refs)
    cq_ref = refs.pop(0) if need_q else None
    ckv_ref, kr_ref = refs.pop(0), refs.pop(0)
    gq_ref = refs.pop(0) if need_q else None
    gkv_ref = refs.pop(0)
    wq_ref = refs.pop(0) if need_q else None
    wkv_ref = refs.pop(0)
    t64_ref = refs.pop(0) if rope else None
    q_ref = refs.pop(0) if need_q else None
    k_ref, v_ref = refs

    def nrm(x_ref, g_ref):
        xf = x_ref[...].astype(F32)
        ms = jnp.mean(xf * xf, axis=-1, keepdims=True)
        return (xf * lax.rsqrt(ms + EPS) * g_ref[...]).astype(BF16)

    kv = jnp.dot(nrm(ckv_ref, gkv_ref), wkv_ref[...], preferred_element_type=F32)
    krb = kr_ref[...].astype(BF16)
    ones = jnp.ones((kv.shape[0], HD), BF16)
    for h in range(B_HEADS):
        lo = h * 2 * HD
        k_ref[:, lo:lo + HD] = kv[:, h * HD:(h + 1) * HD].astype(BF16)
        k_ref[:, lo + HD:lo + 2 * HD] = krb
        v_ref[:, lo:lo + HD] = kv[:, (B_HEADS + h) * HD:(B_HEADS + h + 1) * HD].astype(BF16)
        v_ref[:, lo + HD:lo + 2 * HD] = ones
    if need_q:
        qscale = (NOPE + ROPE) ** -0.5 * LOG2E
        q = jnp.dot(nrm(cq_ref, gq_ref), wq_ref[...], preferred_element_type=F32)
        for h in range(B_HEADS):
            lo = h * 2 * HD
            q_ref[:, lo:lo + HD] = (q[:, lo:lo + HD] * qscale).astype(BF16)
            qr = q[:, lo + HD:lo + 2 * HD]
            if rope:
                qr = _rope(qr, t64_ref, ROPE // 4)
            q_ref[:, lo + HD:lo + 2 * HD] = (qr * qscale).astype(BF16)


def _mla_proj(proj, col0, kr, gq, gkv, wq, wkv, t64, need_q):
    rows = proj.shape[0]
    rope = t64 is not None
    row_blk = lambda w: pl.BlockSpec((TM, w), lambda i: (i, 0))
    const = lambda a: pl.BlockSpec(a.shape, lambda i: (0, 0))
    ins, specs = [], []
    if need_q:
        ins.append(proj)
        specs.append(pl.BlockSpec((TM, LORA), lambda i: (i, (BCQ - col0) // LORA)))
    ins += [proj, kr]
    specs += [pl.BlockSpec((TM, LORA), lambda i: (i, (BCKV - col0) // LORA)), row_blk(128)]
    if need_q:
        ins.append(gq)
        specs.append(const(gq))
    ins.append(gkv)
    specs.append(const(gkv))
    if need_q:
        ins.append(wq)
        specs.append(const(wq))
    ins.append(wkv)
    specs.append(const(wkv))
    if rope:
        ins.append(t64)
        specs.append(pl.BlockSpec((3, TM, 128), lambda i: (0, i % (SEQ // TM), 0)))
    n_out = 3 if need_q else 2
    return pl.pallas_call(
        functools.partial(_mla_proj_kernel, rope=rope, need_q=need_q),
        grid=(rows // TM,),
        in_specs=specs,
        out_specs=[row_blk(2 * W1K)] * n_out,
        out_shape=[jax.ShapeDtypeStruct((rows, 2 * W1K), BF16)] * n_out,
        compiler_params=_cparams(("parallel",)),
        name="mla_proj" if rope else "mla_proj_ctx",
    )(*ins)


def _stack_heads(q):
    return jnp.concatenate([q[:, g * HD:(g + 1) * HD] for g in range(A_GROUP)], axis=0)


def _unstack_heads(o, rows):
    return jnp.concatenate([o[g * rows:(g + 1) * rows, :] for g in range(A_GROUP)], axis=1)


def _sink_col(sink_ref, kvh, rows):
    return jnp.concatenate(
        [jnp.full((rows, HD), sink_ref[kvh * A_GROUP + g] * LOG2E, F32) for g in range(A_GROUP)],
        axis=0)


def _with_ones(v):
    return jnp.concatenate([v, jnp.ones(v.shape, v.dtype)], axis=1)


def _attn_a_lat_kernel(sink_ref, q_ref, kp_ref, kc_ref, kn_ref, vp_ref, vc_ref, vn_ref,
                       kx_ref, vx_ref, o_ref):
    kvh = pl.program_id(1)
    qb = pl.program_id(2)
    nsub = QB_A // WBLK
    kfull = jnp.concatenate([kp_ref[...], kc_ref[...], kn_ref[...]], axis=0)
    vfull = _with_ones(jnp.concatenate([vp_ref[...], vc_ref[...], vn_ref[...]], axis=0))
    kx = kx_ref[...]
    vx = _with_ones(vx_ref[...])
    sink = _sink_col(sink_ref, kvh, WBLK)
    rows = A_GROUP * WBLK
    ti = lax.broadcasted_iota(jnp.int32, (rows, 3 * WBLK), 0) % WBLK
    kj = lax.broadcasted_iota(jnp.int32, (rows, 3 * WBLK), 1)
    rel = kj - ti
    band = jnp.logical_and(rel >= 0, rel <= 2 * WBLK)
    first = qb == 0
    last = qb == SEQ // QB_A - 1
    for s in range(nsub):
        qs = _stack_heads(q_ref[s * WBLK:(s + 1) * WBLK, :])
        kw = kfull[s * WBLK:(s + 3) * WBLK, :]
        vw = vfull[s * WBLK:(s + 3) * WBLK, :]
        valid = band
        if s == 0:
            valid = jnp.logical_and(valid, jnp.logical_not(jnp.logical_and(kj < WBLK, first)))
        if s == nsub - 1:
            valid = jnp.logical_and(valid, jnp.logical_not(jnp.logical_and(kj >= 2 * WBLK, last)))
        s_loc = jnp.where(valid, _nt_dot(qs, kw), -1e30)
        s_ctx = _nt_dot(qs, kx)
        m = jnp.maximum(jnp.maximum(jnp.max(s_loc, axis=-1, keepdims=True),
                                    jnp.max(s_ctx, axis=-1, keepdims=True)), sink)
        p_loc = jnp.exp2(s_loc - m[:, :1]).astype(BF16)
        p_ctx = jnp.exp2(s_ctx - m[:, :1]).astype(BF16)
        res = (jnp.dot(p_loc, vw, preferred_element_type=F32)
               + jnp.dot(p_ctx, vx, preferred_element_type=F32))
        o = res[:, :HD] / (res[:, HD:] + jnp.exp2(sink - m))
        o_ref[s * WBLK:(s + 1) * WBLK, :] = _unstack_heads(o, WBLK).astype(BF16)


def _attn_a_lat(sink, proj, proj_ctx, col0):
    nqb = SEQ // QB_A
    sub = QB_A // WBLK
    nblk = SEQ // WBLK
    row = lambda b, k, q: b * nqb + q
    prev = lambda b, k, q: b * nblk + jnp.maximum(q * sub - 1, 0)
    nxt = lambda b, k, q: b * nblk + jnp.minimum(q * sub + sub, nblk - 1)
    return pl.pallas_call(
        _attn_a_lat_kernel,
        grid=(B, A_KV, nqb),
        in_specs=[
            pl.BlockSpec(memory_space=pltpu.SMEM),
            pl.BlockSpec((QB_A, A_GROUP * HD), lambda b, k, q: (row(b, k, q), AQ // (A_GROUP * HD) + k)),
            pl.BlockSpec((WBLK, HD), lambda b, k, q: (prev(b, k, q), AK // HD + k)),
            pl.BlockSpec((QB_A, HD), lambda b, k, q: (row(b, k, q), AK // HD + k)),
            pl.BlockSpec((WBLK, HD), lambda b, k, q: (nxt(b, k, q), AK // HD + k)),
            pl.BlockSpec((WBLK, HD), lambda b, k, q: (prev(b, k, q), AV // HD + k)),
            pl.BlockSpec((QB_A, HD), lambda b, k, q: (row(b, k, q), AV // HD + k)),
            pl.BlockSpec((WBLK, HD), lambda b, k, q: (nxt(b, k, q), AV // HD + k)),
            pl.BlockSpec((CTX, HD), lambda b, k, q: (b, (AK - col0) // HD + k)),
            pl.BlockSpec((CTX, HD), lambda b, k, q: (b, (AV - col0) // HD + k)),
        ],
        out_specs=pl.BlockSpec((QB_A, A_GROUP * HD), lambda b, k, q: (row(b, k, q), k)),
        out_shape=jax.ShapeDtypeStruct((N_LAT, W1K), BF16),
        compiler_params=_cparams(("parallel", "parallel", "parallel")),
        name="attn_a_lat",
    )(sink, proj, proj, proj, proj, proj, proj, proj, proj_ctx, proj_ctx)


def _attn_a_ctx_kernel(sink_ref, q_ref, kx_ref, vx_ref, o_ref):
    kvh = pl.program_id(1)
    qs = _stack_heads(q_ref[...])
    sink = _sink_col(sink_ref, kvh, CTX)
    s_ctx = _nt_dot(qs, kx_ref[...])
    m = jnp.maximum(jnp.max(s_ctx, axis=-1, keepdims=True), sink)
    p = jnp.exp2(s_ctx - m[:, :1]).astype(BF16)
    res = jnp.dot(p, _with_ones(vx_ref[...]), preferred_element_type=F32)
    o = res[:, :HD] / (res[:, HD:] + jnp.exp2(sink - m))
    o_ref[...] = _unstack_heads(o, CTX).astype(BF16)


def _attn_a_ctx(sink, proj_ctx):
    return pl.pallas_call(
        _attn_a_ctx_kernel,
        grid=(B, A_KV),
        in_specs=[
            pl.BlockSpec(memory_space=pltpu.SMEM),
            pl.BlockSpec((CTX, A_GROUP * HD), lambda b, k: (b, AQ // (A_GROUP * HD) + k)),
            pl.BlockSpec((CTX, HD), lambda b, k: (b, AK // HD + k)),
            pl.BlockSpec((CTX, HD), lambda b, k: (b, AV // HD + k)),
        ],
        out_specs=pl.BlockSpec((CTX, A_GROUP * HD), lambda b, k: (b, k)),
        out_shape=jax.ShapeDtypeStruct((N_CTX, W1K), BF16),
        compiler_params=_cparams(("parallel", "parallel")),
        name="attn_a_ctx",
    )(sink, proj_ctx, proj_ctx, proj_ctx)


def _attn_b_lat_kernel(q_ref, kl_ref, kx_ref, vl_ref, vx_ref, o_ref,
                       sa_ref, ma_ref, sb_ref, mb_ref):
    g = pl.program_id(0)

    @pl.when(g == 0)
    def _():
        sb_ref[...] = jnp.zeros(sb_ref.shape, F32)
        mb_ref[...] = jnp.zeros(mb_ref.shape, F32)

    def step(s_cur, m_cur, s_prev, m_prev):
        q = q_ref[...]
        s1 = _nt_dot(q, kl_ref[...])
        s2 = _nt_dot(q, kx_ref[...])
        s_cur[:, :SEQ] = s1
        s_cur[:, SEQ:] = s2
        m_cur[...] = jnp.maximum(jnp.max(s1, axis=-1, keepdims=True),
                                 jnp.max(s2, axis=-1, keepdims=True))
        m = m_prev[...]
        p1 = jnp.exp2(s_prev[:, :SEQ] - m).astype(BF16)
        p2 = jnp.exp2(s_prev[:, SEQ:] - m).astype(BF16)
        res = (jnp.dot(p1, vl_ref[...], preferred_element_type=F32)
               + jnp.dot(p2, vx_ref[...], preferred_element_type=F32))
        o_ref[...] = (res[:, :VD] / res[:, VD:]).astype(BF16)

    @pl.when(g % 2 == 0)
    def _():
        step(sa_ref, ma_ref, sb_ref, mb_ref)

    @pl.when(g % 2 == 1)
    def _():
        step(sb_ref, mb_ref, sa_ref, ma_ref)


def _attn_b_lat(qcat, kcat, vcat, kcat_ctx, vcat_ctx):
    nq = SEQ // TQ_B
    per_b = B_HEADS * nq
    n = B * per_b
    cur = lambda g: jnp.minimum(g, n - 1)
    prv = lambda g: jnp.maximum(g - 1, 0)
    bat = lambda t: t // per_b
    head = lambda t: (t // nq) % B_HEADS
    qrow = lambda t: bat(t) * nq + t % nq
    return pl.pallas_call(
        _attn_b_lat_kernel,
        grid=(n + 1,),
        in_specs=[
            pl.BlockSpec((TQ_B, 2 * HD), lambda g: (qrow(cur(g)), head(cur(g)))),
            pl.BlockSpec((SEQ, 2 * HD), lambda g: (bat(cur(g)), head(cur(g)))),
            pl.BlockSpec((CTX, 2 * HD), lambda g: (bat(cur(g)), head(cur(g)))),
            pl.BlockSpec((SEQ, 2 * HD), lambda g: (bat(prv(g)), head(prv(g)))),
            pl.BlockSpec((CTX, 2 * HD), lambda g: (bat(prv(g)), head(prv(g)))),
        ],
        out_specs=pl.BlockSpec((TQ_B, VD), lambda g: (qrow(prv(g)), head(prv(g)))),
        out_shape=jax.ShapeDtypeStruct((N_LAT, W1K), BF16),
        scratch_shapes=[pltpu.VMEM((TQ_B, SEQ + CTX), F32), pltpu.VMEM((TQ_B, 1), F32),
                        pltpu.VMEM((TQ_B, SEQ + CTX), F32), pltpu.VMEM((TQ_B, 1), F32)],
        compiler_params=_cparams(("arbitrary",)),
        name="attn_b_lat",
    )(qcat, kcat, kcat_ctx, vcat, vcat_ctx)


def _attn_b_ctx_kernel(q_ref, kx_ref, vx_ref, o_ref):
    s = _nt_dot(q_ref[...], kx_ref[...])
    p = jnp.exp2(s - jnp.max(s, axis=-1, keepdims=True)).astype(BF16)
    res = jnp.dot(p, vx_ref[...], preferred_element_type=F32)
    o_ref[...] = (res[:, :VD] / res[:, VD:]).astype(BF16)


def _attn_b_ctx(qcat, kcat, vcat):
    blk = pl.BlockSpec((CTX, 2 * HD), lambda b, h: (b, h))
    return pl.pallas_call(
        _attn_b_ctx_kernel,
        grid=(B, B_HEADS),
        in_specs=[blk, blk, blk],
        out_specs=pl.BlockSpec((CTX, VD), lambda b, h: (b, h)),
        out_shape=jax.ShapeDtypeStruct((N_CTX, W1K), BF16),
        compiler_params=_cparams(("parallel", "parallel")),
        name="attn_b_ctx",
    )(qcat, kcat, vcat)


def _merge_kernel(x_ref, gate_ref, ya_ref, yb_ref, za_ref, zb_ref, cu_ref, cv_ref, cz_ref,
                  ga_ref, gb_ref, gc_ref, lng_ref, lnb_ref, ws_ref, bs_ref,
                  wpa_ref, wpb_ref, wpc_ref, wout_ref, fg_ref, o_ref, *, final):
    tm = x_ref.shape[0]
    nchunk = tm // CHUNK

    def silu(z_ref):
        z = z_ref[...].astype(F32)
        return z * jax.nn.sigmoid(z)

    def branch(y, z_ref, g_ref, w_ref):
        t = (y * silu(z_ref)).astype(BF16)
        return jax.nn.sigmoid(g_ref[...].astype(F32)) * jnp.dot(
            t, w_ref[...], preferred_element_type=F32)

    v = cv_ref[...].astype(F32)
    mu = jnp.mean(v, axis=-1, keepdims=True)
    vc = v - mu
    var = jnp.mean(vc * vc, axis=-1, keepdims=True)
    vn = (vc * lax.rsqrt(var + EPS) * lng_ref[...] + lnb_ref[...]).astype(BF16)
    cols = []
    for g in range(C_GROUPS):
        gs = slice(g * CHUNK, (g + 1) * CHUNK)
        rhs = jnp.concatenate([vn[c * CHUNK:(c + 1) * CHUNK, gs] for c in range(nchunk)], axis=1)
        mx = jnp.dot(ws_ref[g], rhs, preferred_element_type=F32)
        cols.append(jnp.concatenate(
            [mx[:, c * CHUNK:(c + 1) * CHUNK] for c in range(nchunk)], axis=0))
    bias = jnp.concatenate([bs_ref[...]] * nchunk, axis=0)
    yc = cu_ref[...].astype(F32) * (jnp.concatenate(cols, axis=1) + bias)

    m = (branch(ya_ref[...].astype(F32), za_ref, ga_ref, wpa_ref)
         + branch(yb_ref[...].astype(F32), zb_ref, gb_ref, wpb_ref)
         + branch(yc, cz_ref, gc_ref, wpc_ref))
    out = jnp.dot(m.astype(BF16), wout_ref[...], preferred_element_type=F32)
    xn = x_ref[...] + gate_ref[...] * out
    if final:
        ms = jnp.mean(xn * xn, axis=-1, keepdims=True)
        xn = xn * lax.rsqrt(ms + EPS) * fg_ref[...]
    o_ref[...] = xn


def _merge(x_rows, gate, group0, y_a, y_b, proj, lng, lnb, ws, bs, wpa, wpb, wpc, wout, fg, final):
    tm = TM_MERGE
    seg = lambda c: pl.BlockSpec((tm, W1K), lambda i: (i, c // W1K))
    seg2 = lambda c: pl.BlockSpec((tm, D), lambda i: (i, c // D))
    const = lambda shape: pl.BlockSpec(shape, lambda i: (0,) * len(shape),
                                       pipeline_mode=pl.Buffered(1))
    return pl.pallas_call(
        functools.partial(_merge_kernel, final=final),
        grid=(x_rows.shape[0] // tm,),
        in_specs=[
            pl.BlockSpec((tm, D), lambda i: (i, 0)),
            pl.BlockSpec((None, 1, D), lambda i: (group0 + i // (SEQ // tm), 0, 0)),
            seg(0), seg(0),
            seg(AZ), seg(BZ), seg(CU), seg(CV), seg(CZ),
            seg2(GA), seg2(GB), seg2(GC),
            const((1, W1K)), const((1, W1K)),
            const((C_GROUPS, CHUNK, CHUNK)), const((CHUNK, W1K)),
            const((W1K, D)), const((W1K, D)), const((W1K, D)), const((D, D)),
            const((1, D)),
        ],
        out_specs=pl.BlockSpec((tm, D), lambda i: (i, 0)),
        out_shape=jax.ShapeDtypeStruct(x_rows.shape, F32),
        compiler_params=_cparams(("parallel",)),
        name="merge_final" if final else "merge",
    )(x_rows, gate, y_a, y_b, proj, proj, proj, proj, proj, proj, proj, proj,
      lng, lnb, ws, bs, wpa, wpb, wpc, wout, fg)


def _rope_tables(width):
    t = np.arange(SEQ)
    row = (t // GRID_W).astype(np.float64)
    col = (t % GRID_W).astype(np.float64)
    half = width // 4
    lane = np.arange(128)
    u = lane % (width // 2)
    inv = THETA ** (-(u % half).astype(np.float64) / half)
    pos = np.where((lane // (width // 2))[None, :] == 0, row[:, None], col[:, None])
    ang = pos * inv[None, :]
    active = (lane < width)[None, :]
    first = (u < half)[None, :]
    cos = np.where(active, np.cos(ang), 1.0)
    sin = np.where(active, np.sin(ang), 0.0)
    tab = np.stack([cos, np.where(first, -sin, 0.0), np.where(first, 0.0, sin)])
    return jnp.asarray(tab, dtype=F32)


PIECE = 512
N_ALIGNED = 7
N_PIECES = 27
KR_COL = N_ALIGNED * PIECE


def _prep_w_kernel(a_ref, b_ref, o_ref):
    p = pl.program_id(1)

    @pl.when(p < N_ALIGNED)
    def _():
        o_ref[...] = a_ref[...].astype(BF16)

    @pl.when(p >= N_ALIGNED)
    def _():
        o_ref[:PIECE - ROPE, :] = a_ref[ROPE:, :].astype(BF16)
        o_ref[PIECE - ROPE:, :] = b_ref[...].astype(BF16)


def _prep_w_in(w_in_t):
    def dst(p):
        d = jnp.where(p >= 3, p - 1, p)
        d = jnp.where(p >= 5, p + 20, d)
        d = jnp.where(p == 2, AK // PIECE, d)
        return jnp.where(p >= N_ALIGNED, p - 3, d)

    return pl.pallas_call(
        _prep_w_kernel,
        grid=(DEPTH, N_PIECES),
        in_specs=[
            pl.BlockSpec((None, PIECE, D), lambda l, p: (l, p, 0)),
            pl.BlockSpec((None, ROPE, D), lambda l, p: (l, (p + 1) * (PIECE // ROPE), 0)),
        ],
        out_specs=pl.BlockSpec((None, PIECE, D), lambda l, p: (l, dst(p), 0)),
        out_shape=jax.ShapeDtypeStruct((DEPTH, NPROJ, D), BF16),
        compiler_params=_cparams(("parallel", "parallel")),
        name="prep_w_in",
    )(w_in_t, w_in_t)


def kernel(x, c, ctx, c_ctx, ada_w, ada_b, norm_g, w_in, sink_a, mla_gq, mla_gkv, w_uq, w_ukv,
           sgu_ln_g, sgu_ln_b, sgu_w, sgu_b, w_pa, w_pb, w_pc, w_out, final_g):
    assert x.shape == (B, SEQ, D) and ctx.shape == (B, CTX, D)
    x_lat = x.reshape(N_LAT, D)
    x_ctx = ctx.reshape(N_CTX, D)
    cc = jnp.concatenate([c, c_ctx[None, :], jnp.zeros((8 - B - 1, D), F32)], axis=0)
    mod = _adaln(cc, ada_w, ada_b)
    t128 = _rope_tables(HD)
    t64 = _rope_tables(ROPE)
    w_in_t = jnp.swapaxes(w_in, 1, 2)
    w_t = _prep_w_in(w_in_t)

    for l in range(DEPTH):
        final = l == DEPTH - 1
        shift = mod[l, :, 0:D].reshape(8, 1, D)
        scale = mod[l, :, D:2 * D].reshape(8, 1, D)
        gate = mod[l, :, 2 * D:].reshape(8, 1, D)
        wq = jnp.pad(w_uq[l].reshape(LORA, B_HEADS, NOPE + ROPE),
                     ((0, 0), (0, 0), (0, 2 * HD - NOPE - ROPE))).reshape(LORA, 2 * W1K).astype(BF16)
        wkv = w_ukv[l].reshape(LORA, B_HEADS, 2, HD).transpose(0, 2, 1, 3).reshape(
            LORA, 2 * W1K).astype(BF16)
        bs = jnp.repeat(sgu_b[l].T, CHUNK, axis=1)
        ng = norm_g[l].reshape(1, D)
        gq = mla_gq[l].reshape(1, LORA)
        gkv = mla_gkv[l].reshape(1, LORA)
        merge_w = (sgu_ln_g[l].reshape(1, W1K), sgu_ln_b[l].reshape(1, W1K),
                   sgu_w[l].astype(BF16), bs,
                   w_pa[l].astype(BF16), w_pb[l].astype(BF16), w_pc[l].astype(BF16),
                   w_out[l].astype(BF16), final_g.reshape(1, D))

        col0 = AK if final else 0
        proj_c, kr_c = _inproj(x_ctx, ng, scale, shift, B, w_t, w_in_t, l, None, col0)
        ctx_kv = _mla_proj(proj_c, col0, kr_c, gq, gkv, wq, wkv, None, not final)
        kcat_c, vcat_c = ctx_kv[-2:]

        proj, kr = _inproj(x_lat, ng, scale, shift, 0, w_t, w_in_t, l, (t128, t64), 0)
        qcat, kcat, vcat = _mla_proj(proj, 0, kr, gq, gkv, wq, wkv, t64, True)
        y_a = _attn_a_lat(sink_a[l], proj, proj_c, col0)
        y_b = _attn_b_lat(qcat, kcat, vcat, kcat_c, vcat_c)
        if not final:
            x_ctx = _merge(x_ctx, gate, B, _attn_a_ctx(sink_a[l], proj_c),
                           _attn_b_ctx(ctx_kv[0], kcat_c, vcat_c), proj_c, *merge_w, False)
        x_lat = _merge(x_lat, gate, 0, y_a, y_b, proj, *merge_w, final)
    return x_lat.reshape(B, SEQ, D)
```

```python
import functools

import jax
import jax.numpy as jnp
import numpy as np
from jax import lax
from jax.experimental import pallas as pl
from jax.experimental.pallas import tpu as pltpu

F32 = jnp.float32
BF16 = jnp.bfloat16

D = 2048
B = 4
SEQ = 4096
DEPTH = 2
GRID_W = 64
CTX = 256
HD = 128
THETA = 10000.0
EPS = 1e-6
A_HEADS = 8
A_KV = 2
A_GROUP = A_HEADS // A_KV
WBLK = 128
B_HEADS = 8
LORA = 512
NOPE = 128
ROPE = 64
VD = 128
C_GROUPS = 8
CHUNK = 128
W1K = 1024
LOG2E = 1.4426950408889634

N_LAT = B * SEQ
N_CTX = B * CTX

AQ, AZ, BZ, CU, CV, CZ = 0, 1024, 2048, 3072, 4096, 5120
GA, GB, GC = 6144, 8192, 10240
AK, AV, BCQ, BCKV = 12288, 12544, 12800, 13312
NPROJ = 13824

TM = 1024
TN = 1536
QB_A = 512
TQ_B = 1024
TM_MERGE = 256

V7X_VMEM_BYTES = 64 * 1024 * 1024
VMEM_LIMIT = V7X_VMEM_BYTES - 8 * 1024 * 1024


def _cparams(sem):
    return pltpu.CompilerParams(dimension_semantics=sem, vmem_limit_bytes=VMEM_LIMIT)


def _rope(x, tab_ref, shift):
    return (x * tab_ref[0]
            + pltpu.roll(x, 128 - shift, 1) * tab_ref[1]
            + pltpu.roll(x, shift, 1) * tab_ref[2])


def _sigmoid(x):
    return 0.5 * jnp.tanh(0.5 * x) + 0.5


def _nt_dot(a, b):
    return lax.dot_general(a, b, (((1,), (1,)), ((), ())), preferred_element_type=F32)


def _adaln_kernel(c_ref, w_ref, b_ref, o_ref):
    c = c_ref[...]
    a = (c * jax.nn.sigmoid(c)).astype(BF16)
    o_ref[...] = jnp.dot(a, w_ref[...].astype(BF16), preferred_element_type=F32) + b_ref[...]


def _adaln(cc, ada_w, ada_b):
    tn = 768
    return pl.pallas_call(
        _adaln_kernel,
        grid=(DEPTH, 3 * D // tn),
        in_specs=[
            pl.BlockSpec((8, D), lambda l, j: (0, 0)),
            pl.BlockSpec((None, D, tn), lambda l, j: (l, 0, j)),
            pl.BlockSpec((None, 1, tn), lambda l, j: (l, 0, j)),
        ],
        out_specs=pl.BlockSpec((None, 8, tn), lambda l, j: (l, 0, j)),
        out_shape=jax.ShapeDtypeStruct((DEPTH, 8, 3 * D), F32),
        compiler_params=_cparams(("parallel", "parallel")),
        name="adaln",
    )(cc, ada_w, ada_b.reshape(DEPTH, 1, 3 * D))


SEGMENTS = ((AQ, AZ, "q"), (AZ, CU, "silu"), (CU, CZ, "plain"), (CZ, GA, "silu"),
            (GA, AK, "sigmoid"), (AK, AV, "k"), (AV, NPROJ, "plain"))


def _tile_plan(c0, rope):
    plan = []
    for lo, hi, kind in SEGMENTS:
        lo, hi = max(lo, c0), min(hi, c0 + TN)
        if lo < hi:
            if kind == "k" and not rope:
                kind = "plain"
            if plan and plan[-1][2] == kind and kind in ("plain", "silu", "sigmoid"):
                plan[-1] = (plan[-1][0], hi - c0, kind)
            else:
                plan.append((lo - c0, hi - c0, kind))
    return tuple(plan)


def _inproj_kernel(*refs, rope, col0, n_tiles):
    if rope:
        (x_ref, g_ref, sc_ref, sh_ref, w_ref, wkr_ref, t128_ref, t64_ref,
         o_ref, kr_ref, h_ref) = refs
    else:
        x_ref, g_ref, sc_ref, sh_ref, w_ref, wkr_ref, o_ref, kr_ref, h_ref = refs
    j = pl.program_id(1)

    @pl.when(j == 0)
    def _():
        x = x_ref[...]
        ms = jnp.mean(x * x, axis=-1, keepdims=True)
        h = (x * lax.rsqrt(ms + EPS) * g_ref[...]) * (1.0 + sc_ref[...]) + sh_ref[...]
        hb = h.astype(BF16)
        h_ref[...] = hb
        kr = _nt_dot(hb, wkr_ref[...].astype(BF16))
        kr = jnp.concatenate([kr, jnp.zeros((kr.shape[0], 128 - ROPE), F32)], axis=1)
        kr_ref[...] = _rope(kr, t64_ref, ROPE // 4) if rope else kr

    def store_tile(plan):
        acc = _nt_dot(h_ref[...], w_ref[...])
        for lo, hi, kind in plan:
            if kind in ("q", "k"):
                scale = HD ** -0.5 * LOG2E if kind == "q" else 1.0
                for c in range(lo, hi, HD):
                    blk = acc[:, c:c + HD]
                    if rope:
                        blk = _rope(blk, t128_ref, HD // 4)
                    o_ref[:, c:c + HD] = (blk * scale).astype(BF16)
            else:
                blk = acc[:, lo:hi]
                if kind == "silu":
                    blk = blk * _sigmoid(blk)
                elif kind == "sigmoid":
                    blk = _sigmoid(blk)
                o_ref[:, lo:hi] = blk.astype(BF16)

    tiles_of = {}
    for t in range(n_tiles):
        tiles_of.setdefault(_tile_plan(col0 + t * TN, rope), []).append(t)
    for plan, tiles in tiles_of.items():
        if len(tiles_of) == 1:
            store_tile(plan)
        else:
            cond = functools.reduce(jnp.logical_or, [j == t for t in tiles])
            pl.when(cond)(functools.partial(store_tile, plan))


def _inproj(x_rows, norm_g, scale, shift, group0, w_t, w_in_t, layer, tabs, col0):
    rows = x_rows.shape[0]
    assert rows % TM == 0 and col0 % TN == 0 and NPROJ % TN == 0
    rope = tabs is not None
    tab_idx = lambda i, j: (0, i % (SEQ // TM), 0)
    grp = lambda i, j: (group0 + i // (SEQ // TM), 0, 0)
    in_specs = [
        pl.BlockSpec((TM, D), lambda i, j: (i, 0)),
        pl.BlockSpec((1, D), lambda i, j: (0, 0)),
        pl.BlockSpec((None, 1, D), grp),
        pl.BlockSpec((None, 1, D), grp),
        pl.BlockSpec((None, TN, D), lambda i, j: (layer, j + col0 // TN, 0)),
        pl.BlockSpec((None, ROPE, D), lambda i, j: (layer, KR_COL // ROPE, 0)),
    ]
    if rope:
        in_specs += [pl.BlockSpec((3, TM, 128), tab_idx), pl.BlockSpec((3, TM, 128), tab_idx)]
    return pl.pallas_call(
        functools.partial(_inproj_kernel, rope=rope, col0=col0, n_tiles=(NPROJ - col0) // TN),
        grid=(rows // TM, (NPROJ - col0) // TN),
        in_specs=in_specs,
        out_specs=[
            pl.BlockSpec((TM, TN), lambda i, j: (i, j)),
            pl.BlockSpec((TM, 128), lambda i, j: (i, 0)),
        ],
        out_shape=[
            jax.ShapeDtypeStruct((rows, NPROJ - col0), BF16),
            jax.ShapeDtypeStruct((rows, 128), F32),
        ],
        scratch_shapes=[pltpu.VMEM((TM, D), BF16)],
        compiler_params=_cparams(("parallel", "arbitrary")),
        name="inproj" if rope else "inproj_ctx",
    )(x_rows, norm_g, scale, shift, w_t, w_in_t, *(tabs if rope else ()))


def _mla_proj_kernel(*refs, rope, need_q):
    refs = list(refs)
    cq_ref = refs.pop(0) if need_q else None
    ckv_ref, kr_ref = refs.pop(0), refs.pop(0)
    gq_ref = refs.pop(0) if need_q else None
    gkv_ref = refs.pop(0)
    wq_ref = refs.pop(0) if need_q else None
    wkv_ref = refs.pop(0)
    t64_ref = refs.pop(0) if rope else None
    q_ref = refs.pop(0) if need_q else None
    k_ref, v_ref = refs

    def nrm(x_ref, g_ref):
        xf = x_ref[...].astype(F32)
        ms = jnp.mean(xf * xf, axis=-1, keepdims=True)
        return (xf * lax.rsqrt(ms + EPS) * g_ref[...]).astype(BF16)

    kv = jnp.dot(nrm(ckv_ref, gkv_ref), wkv_ref[...], preferred_element_type=F32)
    krb = kr_ref[...].astype(BF16)
    ones = jnp.ones((kv.shape[0], HD), BF16)
    for h in range(B_HEADS):
        lo = h * 2 * HD
        k_ref[:, lo:lo + HD] = kv[:, h * HD:(h + 1) * HD].astype(BF16)
        k_ref[:, lo + HD:lo + 2 * HD] = krb
        v_ref[:, lo:lo + HD] = kv[:, (B_HEADS + h) * HD:(B_HEADS + h + 1) * HD].astype(BF16)
        v_ref[:, lo + HD:lo + 2 * HD] = ones
    if need_q:
        qscale = (NOPE + ROPE) ** -0.5 * LOG2E
        q = jnp.dot(nrm(cq_ref, gq_ref), wq_ref[...], preferred_element_type=F32)
        for h in range(B_HEADS):
            lo = h * 2 * HD
            q_ref[:, lo:lo + HD] = (q[:, lo:lo + HD] * qscale).astype(BF16)
            qr = q[:, lo + HD:lo + 2 * HD]
            if rope:
                qr = _rope(qr, t64_ref, ROPE // 4)
            q_ref[:, lo + HD:lo + 2 * HD] = (qr * qscale).astype(BF16)


def _mla_proj(proj, col0, kr, gq, gkv, wq, wkv, t64, need_q):
    rows = proj.shape[0]
    rope = t64 is not None
    row_blk = lambda w: pl.BlockSpec((TM, w), lambda i: (i, 0))
    const = lambda a: pl.BlockSpec(a.shape, lambda i: (0, 0))
    ins, specs = [], []
    if need_q:
        ins.append(proj)
        specs.append(pl.BlockSpec((TM, LORA), lambda i: (i, (BCQ - col0) // LORA)))
    ins += [proj, kr]
    specs += [pl.BlockSpec((TM, LORA), lambda i: (i, (BCKV - col0) // LORA)), row_blk(128)]
    if need_q:
        ins.append(gq)
        specs.append(const(gq))
    ins.append(gkv)
    specs.append(const(gkv))
    if need_q:
        ins.append(wq)
        specs.append(const(wq))
    ins.append(wkv)
    specs.append(const(wkv))
    if rope:
        ins.append(t64)
        specs.append(pl.BlockSpec((3, TM, 128), lambda i: (0, i % (SEQ // TM), 0)))
    n_out = 3 if need_q else 2
    return pl.pallas_call(
        functools.partial(_mla_proj_kernel, rope=rope, need_q=need_q),
        grid=(rows // TM,),
        in_specs=specs,
        out_specs=[row_blk(2 * W1K)] * n_out,
        out_shape=[jax.ShapeDtypeStruct((rows, 2 * W1K), BF16)] * n_out,
        compiler_params=_cparams(("parallel",)),
        name="mla_proj" if rope else "mla_proj_ctx",
    )(*ins)


def _stack_heads(q):
    return jnp.concatenate([q[:, g * HD:(g + 1) * HD] for g in range(A_GROUP)], axis=0)


def _unstack_heads(o, rows):
    return jnp.concatenate([o[g * rows:(g + 1) * rows, :] for g in range(A_GROUP)], axis=1)


def _sink_col(sink_ref, kvh, rows):
    return jnp.concatenate(
        [jnp.full((rows, HD), sink_ref[kvh * A_GROUP + g] * LOG2E, F32) for g in range(A_GROUP)],
        axis=0)


def _with_ones(v):
    return jnp.concatenate([v, jnp.ones(v.shape, v.dtype)], axis=1)


def _attn_a_lat_kernel(sink_ref, q_ref, kp_ref, kc_ref, kn_ref, vp_ref, vc_ref, vn_ref,
                       kx_ref, vx_ref, o_ref):
    kvh = pl.program_id(1)
    qb = pl.program_id(2)
    nsub = QB_A // WBLK
    kfull = jnp.concatenate([kp_ref[...], kc_ref[...], kn_ref[...]], axis=0)
    vfull = _with_ones(jnp.concatenate([vp_ref[...], vc_ref[...], vn_ref[...]], axis=0))
    kx = kx_ref[...]
    vx = _with_ones(vx_ref[...])
    sink = _sink_col(sink_ref, kvh, WBLK)
    rows = A_GROUP * WBLK
    ti = lax.broadcasted_iota(jnp.int32, (rows, 3 * WBLK), 0) % WBLK
    kj = lax.broadcasted_iota(jnp.int32, (rows, 3 * WBLK), 1)
    rel = kj - ti
    band = jnp.logical_and(rel >= 0, rel <= 2 * WBLK)
    first = qb == 0
    last = qb == SEQ // QB_A - 1
    for s in range(nsub):
        qs = _stack_heads(q_ref[s * WBLK:(s + 1) * WBLK, :])
        kw = kfull[s * WBLK:(s + 3) * WBLK, :]
        vw = vfull[s * WBLK:(s + 3) * WBLK, :]
        valid = band
        if s == 0:
            valid = jnp.logical_and(valid, jnp.logical_not(jnp.logical_and(kj < WBLK, first)))
        if s == nsub - 1:
            valid = jnp.logical_and(valid, jnp.logical_not(jnp.logical_and(kj >= 2 * WBLK, last)))
        s_loc = jnp.where(valid, _nt_dot(qs, kw), -1e30)
        s_ctx = _nt_dot(qs, kx)
        m = jnp.maximum(jnp.maximum(jnp.max(s_loc, axis=-1, keepdims=True),
                                    jnp.max(s_ctx, axis=-1, keepdims=True)), sink)
        p_loc = jnp.exp2(s_loc - m[:, :1]).astype(BF16)
        p_ctx = jnp.exp2(s_ctx - m[:, :1]).astype(BF16)
        res = (jnp.dot(p_loc, vw, preferred_element_type=F32)
               + jnp.dot(p_ctx, vx, preferred_element_type=F32))
        o = res[:, :HD] / (res[:, HD:] + jnp.exp2(sink - m))
        o_ref[s * WBLK:(s + 1) * WBLK, :] = _unstack_heads(o, WBLK).astype(BF16)


def _attn_a_lat(sink, proj, proj_ctx, col0):
    nqb = SEQ // QB_A
    sub = QB_A // WBLK
    nblk = SEQ // WBLK
    row = lambda b, k, q: b * nqb + q
    prev = lambda b, k, q: b * nblk + jnp.maximum(q * sub - 1, 0)
    nxt = lambda b, k, q: b * nblk + jnp.minimum(q * sub + sub, nblk - 1)
    return pl.pallas_call(
        _attn_a_lat_kernel,
        grid=(B, A_KV, nqb),
        in_specs=[
            pl.BlockSpec(memory_space=pltpu.SMEM),
            pl.BlockSpec((QB_A, A_GROUP * HD), lambda b, k, q: (row(b, k, q), AQ // (A_GROUP * HD) + k)),
            pl.BlockSpec((WBLK, HD), lambda b, k, q: (prev(b, k, q), AK // HD + k)),
            pl.BlockSpec((QB_A, HD), lambda b, k, q: (row(b, k, q), AK // HD + k)),
            pl.BlockSpec((WBLK, HD), lambda b, k, q: (nxt(b, k, q), AK // HD + k)),
            pl.BlockSpec((WBLK, HD), lambda b, k, q: (prev(b, k, q), AV // HD + k)),
            pl.BlockSpec((QB_A, HD), lambda b, k, q: (row(b, k, q), AV // HD + k)),
            pl.BlockSpec((WBLK, HD), lambda b, k, q: (nxt(b, k, q), AV // HD + k)),
            pl.BlockSpec((CTX, HD), lambda b, k, q: (b, (AK - col0) // HD + k)),
            pl.BlockSpec((CTX, HD), lambda b, k, q: (b, (AV - col0) // HD + k)),
        ],
        out_specs=pl.BlockSpec((QB_A, A_GROUP * HD), lambda b, k, q: (row(b, k, q), k)),
        out_shape=jax.ShapeDtypeStruct((N_LAT, W1K), BF16),
        compiler_params=_cparams(("parallel", "parallel", "parallel")),
        name="attn_a_lat",
    )(sink, proj, proj, proj, proj, proj, proj, proj, proj_ctx, proj_ctx)


def _attn_a_ctx_kernel(sink_ref, q_ref, kx_ref, vx_ref, o_ref):
    kvh = pl.program_id(1)
    qs = _stack_heads(q_ref[...])
    sink = _sink_col(sink_ref, kvh, CTX)
    s_ctx = _nt_dot(qs, kx_ref[...])
    m = jnp.maximum(jnp.max(s_ctx, axis=-1, keepdims=True), sink)
    p = jnp.exp2(s_ctx - m[:, :1]).astype(BF16)
    res = jnp.dot(p, _with_ones(vx_ref[...]), preferred_element_type=F32)
    o = res[:, :HD] / (res[:, HD:] + jnp.exp2(sink - m))
    o_ref[...] = _unstack_heads(o, CTX).astype(BF16)


def _attn_a_ctx(sink, proj_ctx):
    return pl.pallas_call(
        _attn_a_ctx_kernel,
        grid=(B, A_KV),
        in_specs=[
            pl.BlockSpec(memory_space=pltpu.SMEM),
            pl.BlockSpec((CTX, A_GROUP * HD), lambda b, k: (b, AQ // (A_GROUP * HD) + k)),
            pl.BlockSpec((CTX, HD), lambda b, k: (b, AK // HD + k)),
            pl.BlockSpec((CTX, HD), lambda b, k: (b, AV // HD + k)),
        ],
        out_specs=pl.BlockSpec((CTX, A_GROUP * HD), lambda b, k: (b, k)),
        out_shape=jax.ShapeDtypeStruct((N_CTX, W1K), BF16),
        compiler_params=_cparams(("parallel", "parallel")),
        name="attn_a_ctx",
    )(sink, proj_ctx, proj_ctx, proj_ctx)


def _attn_b_lat_kernel(q_ref, kl_ref, kx_ref, vl_ref, vx_ref, o_ref,
                       sa_ref, ma_ref, sb_ref, mb_ref):
    g = pl.program_id(0)

    @pl.when(g == 0)
    def _():
        sb_ref[...] = jnp.zeros(sb_ref.shape, F32)
        mb_ref[...] = jnp.zeros(mb_ref.shape, F32)

    def step(s_cur, m_cur, s_prev, m_prev):
        q = q_ref[...]
        s1 = _nt_dot(q, kl_ref[...])
        s2 = _nt_dot(q, kx_ref[...])
        s_cur[:, :SEQ] = s1
        s_cur[:, SEQ:] = s2
        m_cur[...] = jnp.maximum(jnp.max(s1, axis=-1, keepdims=True),
                                 jnp.max(s2, axis=-1, keepdims=True))
        m = m_prev[...]
        p1 = jnp.exp2(s_prev[:, :SEQ] - m).astype(BF16)
        p2 = jnp.exp2(s_prev[:, SEQ:] - m).astype(BF16)
        res = (jnp.dot(p1, vl_ref[...], preferred_element_type=F32)
               + jnp.dot(p2, vx_ref[...], preferred_element_type=F32))
        o_ref[...] = (res[:, :VD] / res[:, VD:]).astype(BF16)

    @pl.when(g % 2 == 0)
    def _():
        step(sa_ref, ma_ref, sb_ref, mb_ref)

    @pl.when(g % 2 == 1)
    def _():
        step(sb_ref, mb_ref, sa_ref, ma_ref)


def _attn_b_lat(qcat, kcat, vcat, kcat_ctx, vcat_ctx):
    nq = SEQ // TQ_B
    per_b = B_HEADS * nq
    n = B * per_b
    cur = lambda g: jnp.minimum(g, n - 1)
    prv = lambda g: jnp.maximum(g - 1, 0)
    bat = lambda t: t // per_b
    head = lambda t: (t // nq) % B_HEADS
    qrow = lambda t: bat(t) * nq + t % nq
    return pl.pallas_call(
        _attn_b_lat_kernel,
        grid=(n + 1,),
        in_specs=[
            pl.BlockSpec((TQ_B, 2 * HD), lambda g: (qrow(cur(g)), head(cur(g)))),
            pl.BlockSpec((SEQ, 2 * HD), lambda g: (bat(cur(g)), head(cur(g)))),
            pl.BlockSpec((CTX, 2 * HD), lambda g: (bat(cur(g)), head(cur(g)))),
            pl.BlockSpec((SEQ, 2 * HD), lambda g: (bat(prv(g)), head(prv(g)))),
            pl.BlockSpec((CTX, 2 * HD), lambda g: (bat(prv(g)), head(prv(g)))),
        ],
        out_specs=pl.BlockSpec((TQ_B, VD), lambda g: (qrow(prv(g)), head(prv(g)))),
        out_shape=jax.ShapeDtypeStruct((N_LAT, W1K), BF16),
        scratch_shapes=[pltpu.VMEM((TQ_B, SEQ + CTX), F32), pltpu.VMEM((TQ_B, 1), F32),
                        pltpu.VMEM((TQ_B, SEQ + CTX), F32), pltpu.VMEM((TQ_B, 1), F32)],
        compiler_params=_cparams(("arbitrary",)),
        name="attn_b_lat",
    )(qcat, kcat, kcat_ctx, vcat, vcat_ctx)


def _attn_b_ctx_kernel(q_ref, kx_ref, vx_ref, o_ref):
    for h in range(B_HEADS):
        hs = slice(h * 2 * HD, (h + 1) * 2 * HD)
        s = _nt_dot(q_ref[:, hs], kx_ref[:, hs])
        p = jnp.exp2(s - jnp.max(s, axis=-1, keepdims=True)).astype(BF16)
        res = jnp.dot(p, vx_ref[:, hs], preferred_element_type=F32)
        o_ref[:, h * VD:(h + 1) * VD] = (res[:, :VD] / res[:, VD:]).astype(BF16)


def _attn_b_ctx(qcat, kcat, vcat):
    blk = pl.BlockSpec((CTX, 2 * W1K), lambda b: (b, 0))
    return pl.pallas_call(
        _attn_b_ctx_kernel,
        grid=(B,),
        in_specs=[blk, blk, blk],
        out_specs=pl.BlockSpec((CTX, W1K), lambda b: (b, 0)),
        out_shape=jax.ShapeDtypeStruct((N_CTX, W1K), BF16),
        compiler_params=_cparams(("parallel",)),
        name="attn_b_ctx",
    )(qcat, kcat, vcat)


def _merge_kernel(x_ref, gate_ref, ya_ref, yb_ref, za_ref, zb_ref, cu_ref, cv_ref, cz_ref,
                  ga_ref, gb_ref, gc_ref, lng_ref, lnb_ref, ws_ref, bs_ref,
                  wpa_ref, wpb_ref, wpc_ref, wout_ref, fg_ref, o_ref, *, final):
    tm = x_ref.shape[0]
    nchunk = tm // CHUNK

    def branch(t, g_ref, w_ref):
        return g_ref[...].astype(F32) * jnp.dot(t, w_ref[...], preferred_element_type=F32)

    v = cv_ref[...].astype(F32)
    mu = jnp.mean(v, axis=-1, keepdims=True)
    vc = v - mu
    var = jnp.mean(vc * vc, axis=-1, keepdims=True)
    vn = (vc * lax.rsqrt(var + EPS) * lng_ref[...] + lnb_ref[...]).astype(BF16)
    cols = []
    for g in range(C_GROUPS):
        gs = slice(g * CHUNK, (g + 1) * CHUNK)
        rhs = jnp.concatenate([vn[c * CHUNK:(c + 1) * CHUNK, gs] for c in range(nchunk)], axis=1)
        mx = jnp.dot(ws_ref[g], rhs, preferred_element_type=F32)
        cols.append(jnp.concatenate(
            [mx[:, c * CHUNK:(c + 1) * CHUNK] for c in range(nchunk)], axis=0))
    bias = jnp.concatenate([bs_ref[...]] * nchunk, axis=0)
    yc = cu_ref[...].astype(F32) * (jnp.concatenate(cols, axis=1) + bias)

    m = (branch(ya_ref[...] * za_ref[...], ga_ref, wpa_ref)
         + branch(yb_ref[...] * zb_ref[...], gb_ref, wpb_ref)
         + branch((yc * cz_ref[...].astype(F32)).astype(BF16), gc_ref, wpc_ref))
    out = jnp.dot(m.astype(BF16), wout_ref[...], preferred_element_type=F32)
    xn = x_ref[...] + gate_ref[...] * out
    if final:
        ms = jnp.mean(xn * xn, axis=-1, keepdims=True)
        xn = xn * lax.rsqrt(ms + EPS) * fg_ref[...]
    o_ref[...] = xn


def _merge(x_rows, gate, group0, y_a, y_b, proj, lng, lnb, ws, bs, wpa, wpb, wpc, wout, fg, final):
    tm = TM_MERGE
    seg = lambda c: pl.BlockSpec((tm, W1K), lambda i: (i, c // W1K))
    seg2 = lambda c: pl.BlockSpec((tm, D), lambda i: (i, c // D))
    const = lambda shape: pl.BlockSpec(shape, lambda i: (0,) * len(shape),
                                       pipeline_mode=pl.Buffered(1))
    return pl.pallas_call(
        functools.partial(_merge_kernel, final=final),
        grid=(x_rows.shape[0] // tm,),
        in_specs=[
            pl.BlockSpec((tm, D), lambda i: (i, 0)),
            pl.BlockSpec((None, 1, D), lambda i: (group0 + i // (SEQ // tm), 0, 0)),
            seg(0), seg(0),
            seg(AZ), seg(BZ), seg(CU), seg(CV), seg(CZ),
            seg2(GA), seg2(GB), seg2(GC),
            const((1, W1K)), const((1, W1K)),
            const((C_GROUPS, CHUNK, CHUNK)), const((CHUNK, W1K)),
            const((W1K, D)), const((W1K, D)), const((W1K, D)), const((D, D)),
            const((1, D)),
        ],
        out_specs=pl.BlockSpec((tm, D), lambda i: (i, 0)),
        out_shape=jax.ShapeDtypeStruct(x_rows.shape, F32),
        compiler_params=_cparams(("parallel",)),
        name="merge_final" if final else "merge",
    )(x_rows, gate, y_a, y_b, proj, proj, proj, proj, proj, proj, proj, proj,
      lng, lnb, ws, bs, wpa, wpb, wpc, wout, fg)


def _rope_tables(width):
    t = np.arange(SEQ)
    row = (t // GRID_W).astype(np.float64)
    col = (t % GRID_W).astype(np.float64)
    half = width // 4
    lane = np.arange(128)
    u = lane % (width // 2)
    inv = THETA ** (-(u % half).astype(np.float64) / half)
    pos = np.where((lane // (width // 2))[None, :] == 0, row[:, None], col[:, None])
    ang = pos * inv[None, :]
    active = (lane < width)[None, :]
    first = (u < half)[None, :]
    cos = np.where(active, np.cos(ang), 1.0)
    sin = np.where(active, np.sin(ang), 0.0)
    tab = np.stack([cos, np.where(first, -sin, 0.0), np.where(first, 0.0, sin)])
    return jnp.asarray(tab, dtype=F32)


PIECE = 512
N_ALIGNED = 7
N_PIECES = 27
KR_COL = N_ALIGNED * PIECE


def _prep_w_kernel(a_ref, b_ref, o_ref):
    p = pl.program_id(1)

    @pl.when(p < N_ALIGNED)
    def _():
        o_ref[...] = a_ref[...].astype(BF16)

    @pl.when(p >= N_ALIGNED)
    def _():
        o_ref[:PIECE - ROPE, :] = a_ref[ROPE:, :].astype(BF16)
        o_ref[PIECE - ROPE:, :] = b_ref[...].astype(BF16)


def _prep_w_in(w_in_t):
    def dst(p):
        d = jnp.where(p >= 3, p - 1, p)
        d = jnp.where(p >= 5, p + 20, d)
        d = jnp.where(p == 2, AK // PIECE, d)
        return jnp.where(p >= N_ALIGNED, p - 3, d)

    return pl.pallas_call(
        _prep_w_kernel,
        grid=(DEPTH, N_PIECES),
        in_specs=[
            pl.BlockSpec((None, PIECE, D), lambda l, p: (l, p, 0)),
            pl.BlockSpec((None, ROPE, D), lambda l, p: (l, (p + 1) * (PIECE // ROPE), 0)),
        ],
        out_specs=pl.BlockSpec((None, PIECE, D), lambda l, p: (l, dst(p), 0)),
        out_shape=jax.ShapeDtypeStruct((DEPTH, NPROJ, D), BF16),
        compiler_params=_cparams(("parallel", "parallel")),
        name="prep_w_in",
    )(w_in_t, w_in_t)


def kernel(x, c, ctx, c_ctx, ada_w, ada_b, norm_g, w_in, sink_a, mla_gq, mla_gkv, w_uq, w_ukv,
           sgu_ln_g, sgu_ln_b, sgu_w, sgu_b, w_pa, w_pb, w_pc, w_out, final_g):
    assert x.shape == (B, SEQ, D) and ctx.shape == (B, CTX, D)
    x_lat = x.reshape(N_LAT, D)
    x_ctx = ctx.reshape(N_CTX, D)
    cc = jnp.concatenate([c, c_ctx[None, :], jnp.zeros((8 - B - 1, D), F32)], axis=0)
    mod = _adaln(cc, ada_w, ada_b)
    t128 = _rope_tables(HD)
    t64 = _rope_tables(ROPE)
    w_in_t = jnp.swapaxes(w_in, 1, 2)
    w_t = _prep_w_in(w_in_t)

    for l in range(DEPTH):
        final = l == DEPTH - 1
        shift = mod[l, :, 0:D].reshape(8, 1, D)
        scale = mod[l, :, D:2 * D].reshape(8, 1, D)
        gate = mod[l, :, 2 * D:].reshape(8, 1, D)
        wq = jnp.pad(w_uq[l].reshape(LORA, B_HEADS, NOPE + ROPE),
                     ((0, 0), (0, 0), (0, 2 * HD - NOPE - ROPE))).reshape(LORA, 2 * W1K).astype(BF16)
        wkv = w_ukv[l].reshape(LORA, B_HEADS, 2, HD).transpose(0, 2, 1, 3).reshape(
            LORA, 2 * W1K).astype(BF16)
        bs = jnp.repeat(sgu_b[l].T, CHUNK, axis=1)
        ng = norm_g[l].reshape(1, D)
        gq = mla_gq[l].reshape(1, LORA)
        gkv = mla_gkv[l].reshape(1, LORA)
        merge_w = (sgu_ln_g[l].reshape(1, W1K), sgu_ln_b[l].reshape(1, W1K),
                   sgu_w[l].astype(BF16), bs,
                   w_pa[l].astype(BF16), w_pb[l].astype(BF16), w_pc[l].astype(BF16),
                   w_out[l].astype(BF16), final_g.reshape(1, D))

        col0 = AK if final else 0
        proj_c, kr_c = _inproj(x_ctx, ng, scale, shift, B, w_t, w_in_t, l, None, col0)
        ctx_kv = _mla_proj(proj_c, col0, kr_c, gq, gkv, wq, wkv, None, not final)
        kcat_c, vcat_c = ctx_kv[-2:]

        proj, kr = _inproj(x_lat, ng, scale, shift, 0, w_t, w_in_t, l, (t128, t64), 0)
        qcat, kcat, vcat = _mla_proj(proj, 0, kr, gq, gkv, wq, wkv, t64, True)
        y_a = _attn_a_lat(sink_a[l], proj, proj_c, col0)
        y_b = _attn_b_lat(qcat, kcat, vcat, kcat_c, vcat_c)
        if not final:
            x_ctx = _merge(x_ctx, gate, B, _attn_a_ctx(sink_a[l], proj_c),
                           _attn_b_ctx(ctx_kv[0], kcat_c, vcat_c), proj_c, *merge_w, False)
        x_lat = _merge(x_lat, gate, 0, y_a, y_b, proj, *merge_w, final)
    return x_lat.reshape(B, SEQ, D)
```

```python
import functools

import jax
import jax.numpy as jnp
import numpy as np
from jax import lax
from jax.experimental import pallas as pl
from jax.experimental.pallas import tpu as pltpu

F32 = jnp.float32
BF16 = jnp.bfloat16

D = 2048
B = 4
SEQ = 4096
DEPTH = 2
GRID_W = 64
CTX = 256
HD = 128
THETA = 10000.0
EPS = 1e-6
A_HEADS = 8
A_KV = 2
A_GROUP = A_HEADS // A_KV
WBLK = 128
B_HEADS = 8
LORA = 512
NOPE = 128
ROPE = 64
VD = 128
C_GROUPS = 8
CHUNK = 128
W1K = 1024
LOG2E = 1.4426950408889634

N_LAT = B * SEQ
N_CTX = B * CTX

AQ, AZ, BZ, CU, CV, CZ = 0, 1024, 2048, 3072, 4096, 5120
GA, GB, GC = 6144, 8192, 10240
AK, AV, BCQ, BCKV = 12288, 12544, 12800, 13312
NPROJ = 13824

TM = 1024
TN = 1536
QB_A = 512
TQ_B = 1024
TM_MERGE = 256

V7X_VMEM_BYTES = 64 * 1024 * 1024
VMEM_LIMIT = V7X_VMEM_BYTES - 8 * 1024 * 1024


def _cparams(sem):
    return pltpu.CompilerParams(dimension_semantics=sem, vmem_limit_bytes=VMEM_LIMIT)


def _rope(x, tab_ref, shift):
    return (x * tab_ref[0]
            + pltpu.roll(x, 128 - shift, 1) * tab_ref[1]
            + pltpu.roll(x, shift, 1) * tab_ref[2])


def _nt_dot(a, b):
    return lax.dot_general(a, b, (((1,), (1,)), ((), ())), preferred_element_type=F32)


def _adaln_kernel(c_ref, w_ref, b_ref, o_ref):
    c = c_ref[...]
    a = (c * jax.nn.sigmoid(c)).astype(BF16)
    o_ref[...] = jnp.dot(a, w_ref[...].astype(BF16), preferred_element_type=F32) + b_ref[...]


def _adaln(cc, ada_w, ada_b):
    tn = 768
    return pl.pallas_call(
        _adaln_kernel,
        grid=(DEPTH, 3 * D // tn),
        in_specs=[
            pl.BlockSpec((8, D), lambda l, j: (0, 0)),
            pl.BlockSpec((None, D, tn), lambda l, j: (l, 0, j)),
            pl.BlockSpec((None, 1, tn), lambda l, j: (l, 0, j)),
        ],
        out_specs=pl.BlockSpec((None, 8, tn), lambda l, j: (l, 0, j)),
        out_shape=jax.ShapeDtypeStruct((DEPTH, 8, 3 * D), F32),
        compiler_params=_cparams(("parallel", "parallel")),
        name="adaln",
    )(cc, ada_w, ada_b.reshape(DEPTH, 1, 3 * D))


SEGMENTS = ((AQ, AZ, "q"), (AZ, AK, "plain"), (AK, AV, "k"), (AV, NPROJ, "plain"))


def _tile_plan(c0, rope):
    plan = []
    for lo, hi, kind in SEGMENTS:
        lo, hi = max(lo, c0), min(hi, c0 + TN)
        if lo < hi:
            if kind == "k" and not rope:
                kind = "plain"
            if plan and plan[-1][2] == kind == "plain":
                plan[-1] = (plan[-1][0], hi - c0, kind)
            else:
                plan.append((lo - c0, hi - c0, kind))
    return tuple(plan)


def _inproj_kernel(*refs, rope, col0, n_tiles):
    if rope:
        (x_ref, g_ref, sc_ref, sh_ref, w_ref, wkr_ref, t128_ref, t64_ref,
         o_ref, kr_ref, h_ref) = refs
    else:
        x_ref, g_ref, sc_ref, sh_ref, w_ref, wkr_ref, o_ref, kr_ref, h_ref = refs
    j = pl.program_id(1)

    @pl.when(j == 0)
    def _():
        x = x_ref[...]
        ms = jnp.mean(x * x, axis=-1, keepdims=True)
        h = (x * lax.rsqrt(ms + EPS) * g_ref[...]) * (1.0 + sc_ref[...]) + sh_ref[...]
        hb = h.astype(BF16)
        h_ref[...] = hb
        kr = _nt_dot(hb, wkr_ref[...].astype(BF16))
        kr = jnp.concatenate([kr, jnp.zeros((kr.shape[0], 128 - ROPE), F32)], axis=1)
        kr_ref[...] = _rope(kr, t64_ref, ROPE // 4) if rope else kr

    def store_tile(plan):
        acc = _nt_dot(h_ref[...], w_ref[...])
        for lo, hi, kind in plan:
            if kind in ("q", "k"):
                scale = HD ** -0.5 * LOG2E if kind == "q" else 1.0
                for c in range(lo, hi, HD):
                    blk = acc[:, c:c + HD]
                    if rope:
                        blk = _rope(blk, t128_ref, HD // 4)
                    o_ref[:, c:c + HD] = (blk * scale).astype(BF16)
            else:
                o_ref[:, lo:hi] = acc[:, lo:hi].astype(BF16)

    tiles_of = {}
    for t in range(n_tiles):
        tiles_of.setdefault(_tile_plan(col0 + t * TN, rope), []).append(t)
    for plan, tiles in tiles_of.items():
        if len(tiles_of) == 1:
            store_tile(plan)
        else:
            cond = functools.reduce(jnp.logical_or, [j == t for t in tiles])
            pl.when(cond)(functools.partial(store_tile, plan))


def _inproj(x_rows, norm_g, scale, shift, group0, w_t, w_in_t, layer, tabs, col0):
    rows = x_rows.shape[0]
    assert rows % TM == 0 and col0 % TN == 0 and NPROJ % TN == 0
    rope = tabs is not None
    tab_idx = lambda i, j: (0, i % (SEQ // TM), 0)
    grp = lambda i, j: (group0 + i // (SEQ // TM), 0, 0)
    in_specs = [
        pl.BlockSpec((TM, D), lambda i, j: (i, 0)),
        pl.BlockSpec((1, D), lambda i, j: (0, 0)),
        pl.BlockSpec((None, 1, D), grp),
        pl.BlockSpec((None, 1, D), grp),
        pl.BlockSpec((None, TN, D), lambda i, j: (layer, j + col0 // TN, 0)),
        pl.BlockSpec((None, ROPE, D), lambda i, j: (layer, KR_COL // ROPE, 0)),
    ]
    if rope:
        in_specs += [pl.BlockSpec((3, TM, 128), tab_idx), pl.BlockSpec((3, TM, 128), tab_idx)]
    return pl.pallas_call(
        functools.partial(_inproj_kernel, rope=rope, col0=col0, n_tiles=(NPROJ - col0) // TN),
        grid=(rows // TM, (NPROJ - col0) // TN),
        in_specs=in_specs,
        out_specs=[
            pl.BlockSpec((TM, TN), lambda i, j: (i, j)),
            pl.BlockSpec((TM, 128), lambda i, j: (i, 0)),
        ],
        out_shape=[
            jax.ShapeDtypeStruct((rows, NPROJ - col0), BF16),
            jax.ShapeDtypeStruct((rows, 128), F32),
        ],
        scratch_shapes=[pltpu.VMEM((TM, D), BF16)],
        compiler_params=_cparams(("parallel", "arbitrary")),
        name="inproj" if rope else "inproj_ctx",
    )(x_rows, norm_g, scale, shift, w_t, w_in_t, *(tabs if rope else ()))


def _mla_proj_kernel(*refs, rope, need_q):
    refs = list(refs)
    cq_ref = refs.pop(0) if need_q else None
    ckv_ref, kr_ref = refs.pop(0), refs.pop(0)
    gq_ref = refs.pop(0) if need_q else None
    gkv_ref = refs.pop(0)
    wq_ref = refs.pop(0) if need_q else None
    wkv_ref = refs.pop(0)
    t64_ref = refs.pop(0) if rope else None
    q_ref = refs.pop(0) if need_q else None
    k_ref, v_ref = refs

    def nrm(x_ref, g_ref):
        xf = x_ref[...].astype(F32)
        ms = jnp.mean(xf * xf, axis=-1, keepdims=True)
        return (xf * lax.rsqrt(ms + EPS) * g_ref[...]).astype(BF16)

    kv = jnp.dot(nrm(ckv_ref, gkv_ref), wkv_ref[...], preferred_element_type=F32)
    krb = kr_ref[...].astype(BF16)
    ones = jnp.ones((kv.shape[0], HD), BF16)
    for h in range(B_HEADS):
        lo = h * 2 * HD
        k_ref[:, lo:lo + HD] = kv[:, h * HD:(h + 1) * HD].astype(BF16)
        k_ref[:, lo + HD:lo + 2 * HD] = krb
        v_ref[:, lo:lo + HD] = kv[:, (B_HEADS + h) * HD:(B_HEADS + h + 1) * HD].astype(BF16)
        v_ref[:, lo + HD:lo + 2 * HD] = ones
    if need_q:
        qscale = (NOPE + ROPE) ** -0.5 * LOG2E
        q = jnp.dot(nrm(cq_ref, gq_ref), wq_ref[...], preferred_element_type=F32)
        for h in range(B_HEADS):
            lo = h * 2 * HD
            q_ref[:, lo:lo + HD] = (q[:, lo:lo + HD] * qscale).astype(BF16)
            qr = q[:, lo + HD:lo + 2 * HD]
            if rope:
                qr = _rope(qr, t64_ref, ROPE // 4)
            q_ref[:, lo + HD:lo + 2 * HD] = (qr * qscale).astype(BF16)


def _mla_proj(proj, col0, kr, gq, gkv, wq, wkv, t64, need_q):
    rows = proj.shape[0]
    rope = t64 is not None
    row_blk = lambda w: pl.BlockSpec((TM, w), lambda i: (i, 0))
    const = lambda a: pl.BlockSpec(a.shape, lambda i: (0, 0))
    ins, specs = [], []
    if need_q:
        ins.append(proj)
        specs.append(pl.BlockSpec((TM, LORA), lambda i: (i, (BCQ - col0) // LORA)))
    ins += [proj, kr]
    specs += [pl.BlockSpec((TM, LORA), lambda i: (i, (BCKV - col0) // LORA)), row_blk(128)]
    if need_q:
        ins.append(gq)
        specs.append(const(gq))
    ins.append(gkv)
    specs.append(const(gkv))
    if need_q:
        ins.append(wq)
        specs.append(const(wq))
    ins.append(wkv)
    specs.append(const(wkv))
    if rope:
        ins.append(t64)
        specs.append(pl.BlockSpec((3, TM, 128), lambda i: (0, i % (SEQ // TM), 0)))
    n_out = 3 if need_q else 2
    return pl.pallas_call(
        functools.partial(_mla_proj_kernel, rope=rope, need_q=need_q),
        grid=(rows // TM,),
        in_specs=specs,
        out_specs=[row_blk(2 * W1K)] * n_out,
        out_shape=[jax.ShapeDtypeStruct((rows, 2 * W1K), BF16)] * n_out,
        compiler_params=_cparams(("parallel",)),
        name="mla_proj" if rope else "mla_proj_ctx",
    )(*ins)


def _stack_heads(q):
    return jnp.concatenate([q[:, g * HD:(g + 1) * HD] for g in range(A_GROUP)], axis=0)


def _unstack_heads(o, rows):
    return jnp.concatenate([o[g * rows:(g + 1) * rows, :] for g in range(A_GROUP)], axis=1)


def _sink_col(sink_ref, kvh, rows):
    return jnp.concatenate(
        [jnp.full((rows, HD), sink_ref[kvh * A_GROUP + g] * LOG2E, F32) for g in range(A_GROUP)],
        axis=0)


def _with_ones(v):
    return jnp.concatenate([v, jnp.ones(v.shape, v.dtype)], axis=1)


def _attn_a_lat_kernel(sink_ref, q_ref, kp_ref, kc_ref, kn_ref, kx_ref,
                       vp_ref, vc_ref, vn_ref, vx_ref, o_ref, bufa, bufb):
    g = pl.program_id(0)
    nqb = SEQ // QB_A
    nsub = QB_A // WBLK
    rows = A_GROUP * WBLK

    @pl.when(g == 0)
    def _():
        for r in bufb:
            r[...] = jnp.zeros(r.shape, F32)

    def scores(buf):
        sl_ref, sx_ref, m_ref, d_ref = buf
        t = jnp.minimum(g, B * A_KV * nqb - 1)
        kvh = (t // nqb) % A_KV
        qb = t % nqb
        kfull = jnp.concatenate([kp_ref[...], kc_ref[...], kn_ref[...]], axis=0)
        kx = kx_ref[...]
        sink = _sink_col(sink_ref, kvh, WBLK)
        ti = lax.broadcasted_iota(jnp.int32, (rows, 3 * WBLK), 0) % WBLK
        kj = lax.broadcasted_iota(jnp.int32, (rows, 3 * WBLK), 1)
        rel = kj - ti
        band = jnp.logical_and(rel >= 0, rel <= 2 * WBLK)
        first = qb == 0
        last = qb == nqb - 1
        for s in range(nsub):
            qs = _stack_heads(q_ref[s * WBLK:(s + 1) * WBLK, :])
            kw = kfull[s * WBLK:(s + 3) * WBLK, :]
            valid = band
            if s == 0:
                valid = jnp.logical_and(valid, jnp.logical_not(jnp.logical_and(kj < WBLK, first)))
            if s == nsub - 1:
                valid = jnp.logical_and(
                    valid, jnp.logical_not(jnp.logical_and(kj >= 2 * WBLK, last)))
            s_loc = jnp.where(valid, _nt_dot(qs, kw), -1e30)
            s_ctx = _nt_dot(qs, kx)
            m = jnp.maximum(jnp.maximum(jnp.max(s_loc, axis=-1, keepdims=True),
                                        jnp.max(s_ctx, axis=-1, keepdims=True)), sink)
            sl_ref[s] = s_loc
            sx_ref[s] = s_ctx
            m_ref[s] = m
            d_ref[s] = jnp.exp2(sink - m)

    def outputs(buf):
        sl_ref, sx_ref, m_ref, d_ref = buf
        vfull = _with_ones(jnp.concatenate([vp_ref[...], vc_ref[...], vn_ref[...]], axis=0))
        vx = _with_ones(vx_ref[...])
        for s in range(nsub):
            vw = vfull[s * WBLK:(s + 3) * WBLK, :]
            m = m_ref[s][:, :1]
            p_loc = jnp.exp2(sl_ref[s] - m).astype(BF16)
            p_ctx = jnp.exp2(sx_ref[s] - m).astype(BF16)
            res = (jnp.dot(p_loc, vw, preferred_element_type=F32)
                   + jnp.dot(p_ctx, vx, preferred_element_type=F32))
            o = res[:, :HD] / (res[:, HD:] + d_ref[s])
            o_ref[s * WBLK:(s + 1) * WBLK, :] = _unstack_heads(o, WBLK).astype(BF16)

    @pl.when(g % 2 == 0)
    def _():
        scores(bufa)
        outputs(bufb)

    @pl.when(g % 2 == 1)
    def _():
        scores(bufb)
        outputs(bufa)


def _attn_a_lat(sink, proj, proj_ctx, col0):
    nqb = SEQ // QB_A
    sub = QB_A // WBLK
    nblk = SEQ // WBLK
    n = B * A_KV * nqb
    cur = lambda g: jnp.minimum(g, n - 1)
    prv = lambda g: jnp.maximum(g - 1, 0)
    bat = lambda t: t // (A_KV * nqb)
    kvh = lambda t: (t // nqb) % A_KV
    qb = lambda t: t % nqb
    row = lambda t: bat(t) * nqb + qb(t)
    prev = lambda t: bat(t) * nblk + jnp.maximum(qb(t) * sub - 1, 0)
    nxt = lambda t: bat(t) * nblk + jnp.minimum(qb(t) * sub + sub, nblk - 1)
    rows = A_GROUP * WBLK

    def kv_specs(col, ctx_col, at):
        return [
            pl.BlockSpec((WBLK, HD), lambda g: (prev(at(g)), col // HD + kvh(at(g)))),
            pl.BlockSpec((QB_A, HD), lambda g: (row(at(g)), col // HD + kvh(at(g)))),
            pl.BlockSpec((WBLK, HD), lambda g: (nxt(at(g)), col // HD + kvh(at(g)))),
            pl.BlockSpec((CTX, HD), lambda g: (bat(at(g)), ctx_col // HD + kvh(at(g)))),
        ]

    def buf():
        return [pltpu.VMEM((sub, rows, 3 * WBLK), F32), pltpu.VMEM((sub, rows, CTX), F32),
                pltpu.VMEM((sub, rows, HD), F32), pltpu.VMEM((sub, rows, HD), F32)]

    return pl.pallas_call(
        _attn_a_lat_kernel,
        grid=(n + 1,),
        in_specs=[
            pl.BlockSpec(memory_space=pltpu.SMEM),
            pl.BlockSpec((QB_A, A_GROUP * HD),
                         lambda g: (row(cur(g)), AQ // (A_GROUP * HD) + kvh(cur(g)))),
            *kv_specs(AK, AK - col0, cur),
            *kv_specs(AV, AV - col0, prv),
        ],
        out_specs=pl.BlockSpec((QB_A, A_GROUP * HD), lambda g: (row(prv(g)), kvh(prv(g)))),
        out_shape=jax.ShapeDtypeStruct((N_LAT, W1K), BF16),
        scratch_shapes=[buf(), buf()],
        compiler_params=_cparams(("arbitrary",)),
        name="attn_a_lat",
    )(sink, proj, proj, proj, proj, proj_ctx, proj, proj, proj, proj_ctx)


def _attn_a_ctx_kernel(sink_ref, q_ref, kx_ref, vx_ref, o_ref):
    kvh = pl.program_id(1)
    qs = _stack_heads(q_ref[...])
    sink = _sink_col(sink_ref, kvh, CTX)
    s_ctx = _nt_dot(qs, kx_ref[...])
    m = jnp.maximum(jnp.max(s_ctx, axis=-1, keepdims=True), sink)
    p = jnp.exp2(s_ctx - m[:, :1]).astype(BF16)
    res = jnp.dot(p, _with_ones(vx_ref[...]), preferred_element_type=F32)
    o = res[:, :HD] / (res[:, HD:] + jnp.exp2(sink - m))
    o_ref[...] = _unstack_heads(o, CTX).astype(BF16)


def _attn_a_ctx(sink, proj_ctx):
    return pl.pallas_call(
        _attn_a_ctx_kernel,
        grid=(B, A_KV),
        in_specs=[
            pl.BlockSpec(memory_space=pltpu.SMEM),
            pl.BlockSpec((CTX, A_GROUP * HD), lambda b, k: (b, AQ // (A_GROUP * HD) + k)),
            pl.BlockSpec((CTX, HD), lambda b, k: (b, AK // HD + k)),
            pl.BlockSpec((CTX, HD), lambda b, k: (b, AV // HD + k)),
        ],
        out_specs=pl.BlockSpec((CTX, A_GROUP * HD), lambda b, k: (b, k)),
        out_shape=jax.ShapeDtypeStruct((N_CTX, W1K), BF16),
        compiler_params=_cparams(("parallel", "parallel")),
        name="attn_a_ctx",
    )(sink, proj_ctx, proj_ctx, proj_ctx)


def _attn_b_lat_kernel(q_ref, kl_ref, kx_ref, vl_ref, vx_ref, o_ref,
                       sa_ref, ma_ref, sb_ref, mb_ref):
    g = pl.program_id(0)

    @pl.when(g == 0)
    def _():
        sb_ref[...] = jnp.zeros(sb_ref.shape, F32)
        mb_ref[...] = jnp.zeros(mb_ref.shape, F32)

    def step(s_cur, m_cur, s_prev, m_prev):
        q = q_ref[...]
        s1 = _nt_dot(q, kl_ref[...])
        s2 = _nt_dot(q, kx_ref[...])
        s_cur[:, :SEQ] = s1
        s_cur[:, SEQ:] = s2
        m_cur[...] = jnp.maximum(jnp.max(s1, axis=-1, keepdims=True),
                                 jnp.max(s2, axis=-1, keepdims=True))
        m = m_prev[...]
        p1 = jnp.exp2(s_prev[:, :SEQ] - m).astype(BF16)
        p2 = jnp.exp2(s_prev[:, SEQ:] - m).astype(BF16)
        res = (jnp.dot(p1, vl_ref[...], preferred_element_type=F32)
               + jnp.dot(p2, vx_ref[...], preferred_element_type=F32))
        o_ref[...] = (res[:, :VD] / res[:, VD:]).astype(BF16)

    @pl.when(g % 2 == 0)
    def _():
        step(sa_ref, ma_ref, sb_ref, mb_ref)

    @pl.when(g % 2 == 1)
    def _():
        step(sb_ref, mb_ref, sa_ref, ma_ref)


def _attn_b_lat(qcat, kcat, vcat, kcat_ctx, vcat_ctx):
    nq = SEQ // TQ_B
    per_b = B_HEADS * nq
    n = B * per_b
    cur = lambda g: jnp.minimum(g, n - 1)
    prv = lambda g: jnp.maximum(g - 1, 0)
    bat = lambda t: t // per_b
    head = lambda t: (t // nq) % B_HEADS
    qrow = lambda t: bat(t) * nq + t % nq
    return pl.pallas_call(
        _attn_b_lat_kernel,
        grid=(n + 1,),
        in_specs=[
            pl.BlockSpec((TQ_B, 2 * HD), lambda g: (qrow(cur(g)), head(cur(g)))),
            pl.BlockSpec((SEQ, 2 * HD), lambda g: (bat(cur(g)), head(cur(g)))),
            pl.BlockSpec((CTX, 2 * HD), lambda g: (bat(cur(g)), head(cur(g)))),
            pl.BlockSpec((SEQ, 2 * HD), lambda g: (bat(prv(g)), head(prv(g)))),
            pl.BlockSpec((CTX, 2 * HD), lambda g: (bat(prv(g)), head(prv(g)))),
        ],
        out_specs=pl.BlockSpec((TQ_B, VD), lambda g: (qrow(prv(g)), head(prv(g)))),
        out_shape=jax.ShapeDtypeStruct((N_LAT, W1K), BF16),
        scratch_shapes=[pltpu.VMEM((TQ_B, SEQ + CTX), F32), pltpu.VMEM((TQ_B, 1), F32),
                        pltpu.VMEM((TQ_B, SEQ + CTX), F32), pltpu.VMEM((TQ_B, 1), F32)],
        compiler_params=_cparams(("arbitrary",)),
        name="attn_b_lat",
    )(qcat, kcat, kcat_ctx, vcat, vcat_ctx)


def _attn_b_ctx_kernel(q_ref, kx_ref, vx_ref, o_ref):
    for h in range(B_HEADS):
        hs = slice(h * 2 * HD, (h + 1) * 2 * HD)
        s = _nt_dot(q_ref[:, hs], kx_ref[:, hs])
        p = jnp.exp2(s - jnp.max(s, axis=-1, keepdims=True)).astype(BF16)
        res = jnp.dot(p, vx_ref[:, hs], preferred_element_type=F32)
        o_ref[:, h * VD:(h + 1) * VD] = (res[:, :VD] / res[:, VD:]).astype(BF16)


def _attn_b_ctx(qcat, kcat, vcat):
    blk = pl.BlockSpec((CTX, 2 * W1K), lambda b: (b, 0))
    return pl.pallas_call(
        _attn_b_ctx_kernel,
        grid=(B,),
        in_specs=[blk, blk, blk],
        out_specs=pl.BlockSpec((CTX, W1K), lambda b: (b, 0)),
        out_shape=jax.ShapeDtypeStruct((N_CTX, W1K), BF16),
        compiler_params=_cparams(("parallel",)),
        name="attn_b_ctx",
    )(qcat, kcat, vcat)


def _merge_kernel(x_ref, gate_ref, ya_ref, yb_ref, za_ref, zb_ref, cu_ref, cv_ref, cz_ref,
                  ga_ref, gb_ref, gc_ref, lng_ref, lnb_ref, ws_ref, bs_ref,
                  wpa_ref, wpb_ref, wpc_ref, wout_ref, fg_ref, o_ref, *, final):
    tm = x_ref.shape[0]
    nchunk = tm // CHUNK

    def silu(z_ref):
        z = z_ref[...].astype(F32)
        return z * jax.nn.sigmoid(z)

    def branch(y, z_ref, g_ref, w_ref):
        t = (y * silu(z_ref)).astype(BF16)
        return jax.nn.sigmoid(g_ref[...].astype(F32)) * jnp.dot(
            t, w_ref[...], preferred_element_type=F32)

    v = cv_ref[...].astype(F32)
    mu = jnp.mean(v, axis=-1, keepdims=True)
    vc = v - mu
    var = jnp.mean(vc * vc, axis=-1, keepdims=True)
    vn = (vc * lax.rsqrt(var + EPS) * lng_ref[...] + lnb_ref[...]).astype(BF16)
    cols = []
    for g in range(C_GROUPS):
        gs = slice(g * CHUNK, (g + 1) * CHUNK)
        rhs = jnp.concatenate([vn[c * CHUNK:(c + 1) * CHUNK, gs] for c in range(nchunk)], axis=1)
        mx = jnp.dot(ws_ref[g], rhs, preferred_element_type=F32)
        cols.append(jnp.concatenate(
            [mx[:, c * CHUNK:(c + 1) * CHUNK] for c in range(nchunk)], axis=0))
    bias = jnp.concatenate([bs_ref[...]] * nchunk, axis=0)
    yc = cu_ref[...].astype(F32) * (jnp.concatenate(cols, axis=1) + bias)

    m = (branch(ya_ref[...].astype(F32), za_ref, ga_ref, wpa_ref)
         + branch(yb_ref[...].astype(F32), zb_ref, gb_ref, wpb_ref)
         + branch(yc, cz_ref, gc_ref, wpc_ref))
    out = jnp.dot(m.astype(BF16), wout_ref[...], preferred_element_type=F32)
    xn = x_ref[...] + gate_ref[...] * out
    if final:
        ms = jnp.mean(xn * xn, axis=-1, keepdims=True)
        xn = xn * lax.rsqrt(ms + EPS) * fg_ref[...]
    o_ref[...] = xn


def _merge(x_rows, gate, group0, y_a, y_b, proj, lng, lnb, ws, bs, wpa, wpb, wpc, wout, fg, final):
    tm = TM_MERGE
    seg = lambda c: pl.BlockSpec((tm, W1K), lambda i: (i, c // W1K))
    seg2 = lambda c: pl.BlockSpec((tm, D), lambda i: (i, c // D))
    const = lambda shape: pl.BlockSpec(shape, lambda i: (0,) * len(shape),
                                       pipeline_mode=pl.Buffered(1))
    return pl.pallas_call(
        functools.partial(_merge_kernel, final=final),
        grid=(x_rows.shape[0] // tm,),
        in_specs=[
            pl.BlockSpec((tm, D), lambda i: (i, 0)),
            pl.BlockSpec((None, 1, D), lambda i: (group0 + i // (SEQ // tm), 0, 0)),
            seg(0), seg(0),
            seg(AZ), seg(BZ), seg(CU), seg(CV), seg(CZ),
            seg2(GA), seg2(GB), seg2(GC),
            const((1, W1K)), const((1, W1K)),
            const((C_GROUPS, CHUNK, CHUNK)), const((CHUNK, W1K)),
            const((W1K, D)), const((W1K, D)), const((W1K, D)), const((D, D)),
            const((1, D)),
        ],
        out_specs=pl.BlockSpec((tm, D), lambda i: (i, 0)),
        out_shape=jax.ShapeDtypeStruct(x_rows.shape, F32),
        compiler_params=_cparams(("parallel",)),
        name="merge_final" if final else "merge",
    )(x_rows, gate, y_a, y_b, proj, proj, proj, proj, proj, proj, proj, proj,
      lng, lnb, ws, bs, wpa, wpb, wpc, wout, fg)


def _rope_tables(width):
    t = np.arange(SEQ)
    row = (t // GRID_W).astype(np.float64)
    col = (t % GRID_W).astype(np.float64)
    half = width // 4
    lane = np.arange(128)
    u = lane % (width // 2)
    inv = THETA ** (-(u % half).astype(np.float64) / half)
    pos = np.where((lane // (width // 2))[None, :] == 0, row[:, None], col[:, None])
    ang = pos * inv[None, :]
    active = (lane < width)[None, :]
    first = (u < half)[None, :]
    cos = np.where(active, np.cos(ang), 1.0)
    sin = np.where(active, np.sin(ang), 0.0)
    tab = np.stack([cos, np.where(first, -sin, 0.0), np.where(first, 0.0, sin)])
    return jnp.asarray(tab, dtype=F32)


PIECE = 512
N_ALIGNED = 7
N_PIECES = 27
KR_COL = N_ALIGNED * PIECE


def _prep_w_kernel(a_ref, b_ref, o_ref):
    p = pl.program_id(1)

    @pl.when(p < N_ALIGNED)
    def _():
        o_ref[...] = a_ref[...].astype(BF16)

    @pl.when(p >= N_ALIGNED)
    def _():
        o_ref[:PIECE - ROPE, :] = a_ref[ROPE:, :].astype(BF16)
        o_ref[PIECE - ROPE:, :] = b_ref[...].astype(BF16)


def _prep_w_in(w_in_t):
    def dst(p):
        d = jnp.where(p >= 3, p - 1, p)
        d = jnp.where(p >= 5, p + 20, d)
        d = jnp.where(p == 2, AK // PIECE, d)
        return jnp.where(p >= N_ALIGNED, p - 3, d)

    return pl.pallas_call(
        _prep_w_kernel,
        grid=(DEPTH, N_PIECES),
        in_specs=[
            pl.BlockSpec((None, PIECE, D), lambda l, p: (l, p, 0)),
            pl.BlockSpec((None, ROPE, D), lambda l, p: (l, (p + 1) * (PIECE // ROPE), 0)),
        ],
        out_specs=pl.BlockSpec((None, PIECE, D), lambda l, p: (l, dst(p), 0)),
        out_shape=jax.ShapeDtypeStruct((DEPTH, NPROJ, D), BF16),
        compiler_params=_cparams(("parallel", "parallel")),
        name="prep_w_in",
    )(w_in_t, w_in_t)


def kernel(x, c, ctx, c_ctx, ada_w, ada_b, norm_g, w_in, sink_a, mla_gq, mla_gkv, w_uq, w_ukv,
           sgu_ln_g, sgu_ln_b, sgu_w, sgu_b, w_pa, w_pb, w_pc, w_out, final_g):
    assert x.shape == (B, SEQ, D) and ctx.shape == (B, CTX, D)
    x_lat = x.reshape(N_LAT, D)
    x_ctx = ctx.reshape(N_CTX, D)
    cc = jnp.concatenate([c, c_ctx[None, :], jnp.zeros((8 - B - 1, D), F32)], axis=0)
    mod = _adaln(cc, ada_w, ada_b)
    t128 = _rope_tables(HD)
    t64 = _rope_tables(ROPE)
    w_in_t = jnp.swapaxes(w_in, 1, 2)
    w_t = _prep_w_in(w_in_t)

    for l in range(DEPTH):
        final = l == DEPTH - 1
        shift = mod[l, :, 0:D].reshape(8, 1, D)
        scale = mod[l, :, D:2 * D].reshape(8, 1, D)
        gate = mod[l, :, 2 * D:].reshape(8, 1, D)
        wq = jnp.pad(w_uq[l].reshape(LORA, B_HEADS, NOPE + ROPE),
                     ((0, 0), (0, 0), (0, 2 * HD - NOPE - ROPE))).reshape(LORA, 2 * W1K).astype(BF16)
        wkv = w_ukv[l].reshape(LORA, B_HEADS, 2, HD).transpose(0, 2, 1, 3).reshape(
            LORA, 2 * W1K).astype(BF16)
        bs = jnp.repeat(sgu_b[l].T, CHUNK, axis=1)
        ng = norm_g[l].reshape(1, D)
        gq = mla_gq[l].reshape(1, LORA)
        gkv = mla_gkv[l].reshape(1, LORA)
        merge_w = (sgu_ln_g[l].reshape(1, W1K), sgu_ln_b[l].reshape(1, W1K),
                   sgu_w[l].astype(BF16), bs,
                   w_pa[l].astype(BF16), w_pb[l].astype(BF16), w_pc[l].astype(BF16),
                   w_out[l].astype(BF16), final_g.reshape(1, D))

        col0 = AK if final else 0
        proj_c, kr_c = _inproj(x_ctx, ng, scale, shift, B, w_t, w_in_t, l, None, col0)
        ctx_kv = _mla_proj(proj_c, col0, kr_c, gq, gkv, wq, wkv, None, not final)
        kcat_c, vcat_c = ctx_kv[-2:]

        proj, kr = _inproj(x_lat, ng, scale, shift, 0, w_t, w_in_t, l, (t128, t64), 0)
        qcat, kcat, vcat = _mla_proj(proj, 0, kr, gq, gkv, wq, wkv, t64, True)
        y_a = _attn_a_lat(sink_a[l], proj, proj_c, col0)
        y_b = _attn_b_lat(qcat, kcat, vcat, kcat_c, vcat_c)
        if not final:
            x_ctx = _merge(x_ctx, gate, B, _attn_a_ctx(sink_a[l], proj_c),
                           _attn_b_ctx(ctx_kv[0], kcat_c, vcat_c), proj_c, *merge_w, False)
        x_lat = _merge(x_lat, gate, 0, y_a, y_b, proj, *merge_w, final)
    return x_lat.reshape(B, SEQ, D)
```

```python
import functools

import jax
import jax.numpy as jnp
import numpy as np
from jax import lax
from jax.experimental import pallas as pl
from jax.experimental.pallas import tpu as pltpu

F32 = jnp.float32
BF16 = jnp.bfloat16

D = 2048
B = 4
SEQ = 4096
DEPTH = 2
GRID_W = 64
CTX = 256
HD = 128
THETA = 10000.0
EPS = 1e-6
A_HEADS = 8
A_KV = 2
A_GROUP = A_HEADS // A_KV
WBLK = 128
B_HEADS = 8
LORA = 512
NOPE = 128
ROPE = 64
VD = 128
C_GROUPS = 8
CHUNK = 128
W1K = 1024
LOG2E = 1.4426950408889634

N_LAT = B * SEQ
N_CTX = B * CTX

AQ, AZ, BZ, CU, CV, CZ = 0, 1024, 2048, 3072, 4096, 5120
GA, GB, GC = 6144, 8192, 10240
AK, AV, BCQ, BCKV = 12288, 12544, 12800, 13312
NPROJ = 13824

TM = 1024
TN = 1536
QB_A = 1024
TQ_B = 1024
TM_MERGE = 256

V7X_VMEM_BYTES = 64 * 1024 * 1024
VMEM_LIMIT = V7X_VMEM_BYTES - 8 * 1024 * 1024


def _cparams(sem):
    return pltpu.CompilerParams(dimension_semantics=sem, vmem_limit_bytes=VMEM_LIMIT)


def _rope(x, tab_ref, shift):
    return (x * tab_ref[0]
            + pltpu.roll(x, 128 - shift, 1) * tab_ref[1]
            + pltpu.roll(x, shift, 1) * tab_ref[2])


def _nt_dot(a, b):
    return lax.dot_general(a, b, (((1,), (1,)), ((), ())), preferred_element_type=F32)


def _adaln_kernel(c_ref, w_ref, b_ref, o_ref):
    c = c_ref[...]
    a = (c * jax.nn.sigmoid(c)).astype(BF16)
    o_ref[...] = jnp.dot(a, w_ref[...].astype(BF16), preferred_element_type=F32) + b_ref[...]


def _adaln(cc, ada_w, ada_b):
    tn = 768
    return pl.pallas_call(
        _adaln_kernel,
        grid=(DEPTH, 3 * D // tn),
        in_specs=[
            pl.BlockSpec((8, D), lambda l, j: (0, 0)),
            pl.BlockSpec((None, D, tn), lambda l, j: (l, 0, j)),
            pl.BlockSpec((None, 1, tn), lambda l, j: (l, 0, j)),
        ],
        out_specs=pl.BlockSpec((None, 8, tn), lambda l, j: (l, 0, j)),
        out_shape=jax.ShapeDtypeStruct((DEPTH, 8, 3 * D), F32),
        compiler_params=_cparams(("parallel", "parallel")),
        name="adaln",
    )(cc, ada_w, ada_b.reshape(DEPTH, 1, 3 * D))


SEGMENTS = ((AQ, AZ, "q"), (AZ, AK, "plain"), (AK, AV, "k"), (AV, NPROJ, "plain"))


def _tile_plan(c0, rope):
    plan = []
    for lo, hi, kind in SEGMENTS:
        lo, hi = max(lo, c0), min(hi, c0 + TN)
        if lo < hi:
            if kind == "k" and not rope:
                kind = "plain"
            if plan and plan[-1][2] == kind == "plain":
                plan[-1] = (plan[-1][0], hi - c0, kind)
            else:
                plan.append((lo - c0, hi - c0, kind))
    return tuple(plan)


def _inproj_kernel(*refs, rope, col0, n_tiles):
    if rope:
        (x_ref, g_ref, sc_ref, sh_ref, w_ref, wkr_ref, t128_ref, t64_ref,
         o_ref, kr_ref, h_ref) = refs
    else:
        x_ref, g_ref, sc_ref, sh_ref, w_ref, wkr_ref, o_ref, kr_ref, h_ref = refs
    j = pl.program_id(1)

    @pl.when(j == 0)
    def _():
        x = x_ref[...]
        ms = jnp.mean(x * x, axis=-1, keepdims=True)
        h = (x * lax.rsqrt(ms + EPS) * g_ref[...]) * (1.0 + sc_ref[...]) + sh_ref[...]
        hb = h.astype(BF16)
        h_ref[...] = hb
        kr = _nt_dot(hb, wkr_ref[...].astype(BF16))
        kr = jnp.concatenate([kr, jnp.zeros((kr.shape[0], 128 - ROPE), F32)], axis=1)
        kr_ref[...] = _rope(kr, t64_ref, ROPE // 4) if rope else kr

    def store_tile(plan):
        acc = _nt_dot(h_ref[...], w_ref[...])
        for lo, hi, kind in plan:
            if kind in ("q", "k"):
                scale = HD ** -0.5 * LOG2E if kind == "q" else 1.0
                for c in range(lo, hi, HD):
                    blk = acc[:, c:c + HD]
                    if rope:
                        blk = _rope(blk, t128_ref, HD // 4)
                    o_ref[:, c:c + HD] = (blk * scale).astype(BF16)
            else:
                o_ref[:, lo:hi] = acc[:, lo:hi].astype(BF16)

    tiles_of = {}
    for t in range(n_tiles):
        tiles_of.setdefault(_tile_plan(col0 + t * TN, rope), []).append(t)
    for plan, tiles in tiles_of.items():
        if len(tiles_of) == 1:
            store_tile(plan)
        else:
            cond = functools.reduce(jnp.logical_or, [j == t for t in tiles])
            pl.when(cond)(functools.partial(store_tile, plan))


def _inproj(x_rows, norm_g, scale, shift, group0, w_t, w_in_t, layer, tabs, col0):
    rows = x_rows.shape[0]
    assert rows % TM == 0 and col0 % TN == 0 and NPROJ % TN == 0
    rope = tabs is not None
    tab_idx = lambda i, j: (0, i % (SEQ // TM), 0)
    grp = lambda i, j: (group0 + i // (SEQ // TM), 0, 0)
    in_specs = [
        pl.BlockSpec((TM, D), lambda i, j: (i, 0)),
        pl.BlockSpec((1, D), lambda i, j: (0, 0)),
        pl.BlockSpec((None, 1, D), grp),
        pl.BlockSpec((None, 1, D), grp),
        pl.BlockSpec((None, TN, D), lambda i, j: (layer, j + col0 // TN, 0)),
        pl.BlockSpec((None, ROPE, D), lambda i, j: (layer, KR_COL // ROPE, 0)),
    ]
    if rope:
        in_specs += [pl.BlockSpec((3, TM, 128), tab_idx), pl.BlockSpec((3, TM, 128), tab_idx)]
    return pl.pallas_call(
        functools.partial(_inproj_kernel, rope=rope, col0=col0, n_tiles=(NPROJ - col0) // TN),
        grid=(rows // TM, (NPROJ - col0) // TN),
        in_specs=in_specs,
        out_specs=[
            pl.BlockSpec((TM, TN), lambda i, j: (i, j)),
            pl.BlockSpec((TM, 128), lambda i, j: (i, 0)),
        ],
        out_shape=[
            jax.ShapeDtypeStruct((rows, NPROJ - col0), BF16),
            jax.ShapeDtypeStruct((rows, 128), F32),
        ],
        scratch_shapes=[pltpu.VMEM((TM, D), BF16)],
        compiler_params=_cparams(("parallel", "arbitrary")),
        name="inproj" if rope else "inproj_ctx",
    )(x_rows, norm_g, scale, shift, w_t, w_in_t, *(tabs if rope else ()))


def _mla_proj_kernel(*refs, rope, need_q):
    refs = list(refs)
    cq_ref = refs.pop(0) if need_q else None
    ckv_ref, kr_ref = refs.pop(0), refs.pop(0)
    gq_ref = refs.pop(0) if need_q else None
    gkv_ref = refs.pop(0)
    wq_ref = refs.pop(0) if need_q else None
    wkv_ref = refs.pop(0)
    t64_ref = refs.pop(0) if rope else None
    q_ref = refs.pop(0) if need_q else None
    k_ref, v_ref = refs

    def nrm(x_ref, g_ref):
        xf = x_ref[...].astype(F32)
        ms = jnp.mean(xf * xf, axis=-1, keepdims=True)
        return (xf * lax.rsqrt(ms + EPS) * g_ref[...]).astype(BF16)

    kv = jnp.dot(nrm(ckv_ref, gkv_ref), wkv_ref[...], preferred_element_type=F32)
    krb = kr_ref[...].astype(BF16)
    ones = jnp.ones((kv.shape[0], HD), BF16)
    for h in range(B_HEADS):
        lo = h * 2 * HD
        k_ref[:, lo:lo + HD] = kv[:, h * HD:(h + 1) * HD].astype(BF16)
        k_ref[:, lo + HD:lo + 2 * HD] = krb
        v_ref[:, lo:lo + HD] = kv[:, (B_HEADS + h) * HD:(B_HEADS + h + 1) * HD].astype(BF16)
        v_ref[:, lo + HD:lo + 2 * HD] = ones
    if need_q:
        qscale = (NOPE + ROPE) ** -0.5 * LOG2E
        q = jnp.dot(nrm(cq_ref, gq_ref), wq_ref[...], preferred_element_type=F32)
        for h in range(B_HEADS):
            lo = h * 2 * HD
            q_ref[:, lo:lo + HD] = (q[:, lo:lo + HD] * qscale).astype(BF16)
            qr = q[:, lo + HD:lo + 2 * HD]
            if rope:
                qr = _rope(qr, t64_ref, ROPE // 4)
            q_ref[:, lo + HD:lo + 2 * HD] = (qr * qscale).astype(BF16)


def _mla_proj(proj, col0, kr, gq, gkv, wq, wkv, t64, need_q):
    rows = proj.shape[0]
    rope = t64 is not None
    row_blk = lambda w: pl.BlockSpec((TM, w), lambda i: (i, 0))
    const = lambda a: pl.BlockSpec(a.shape, lambda i: (0, 0))
    ins, specs = [], []
    if need_q:
        ins.append(proj)
        specs.append(pl.BlockSpec((TM, LORA), lambda i: (i, (BCQ - col0) // LORA)))
    ins += [proj, kr]
    specs += [pl.BlockSpec((TM, LORA), lambda i: (i, (BCKV - col0) // LORA)), row_blk(128)]
    if need_q:
        ins.append(gq)
        specs.append(const(gq))
    ins.append(gkv)
    specs.append(const(gkv))
    if need_q:
        ins.append(wq)
        specs.append(const(wq))
    ins.append(wkv)
    specs.append(const(wkv))
    if rope:
        ins.append(t64)
        specs.append(pl.BlockSpec((3, TM, 128), lambda i: (0, i % (SEQ // TM), 0)))
    n_out = 3 if need_q else 2
    return pl.pallas_call(
        functools.partial(_mla_proj_kernel, rope=rope, need_q=need_q),
        grid=(rows // TM,),
        in_specs=specs,
        out_specs=[row_blk(2 * W1K)] * n_out,
        out_shape=[jax.ShapeDtypeStruct((rows, 2 * W1K), BF16)] * n_out,
        compiler_params=_cparams(("parallel",)),
        name="mla_proj" if rope else "mla_proj_ctx",
    )(*ins)


def _stack_heads(q):
    return jnp.concatenate([q[:, g * HD:(g + 1) * HD] for g in range(A_GROUP)], axis=0)


def _unstack_heads(o, rows):
    return jnp.concatenate([o[g * rows:(g + 1) * rows, :] for g in range(A_GROUP)], axis=1)


def _sink_col(sink_ref, kvh, rows):
    return jnp.concatenate(
        [jnp.full((rows, HD), sink_ref[kvh * A_GROUP + g] * LOG2E, F32) for g in range(A_GROUP)],
        axis=0)


def _with_ones(v):
    return jnp.concatenate([v, jnp.ones(v.shape, v.dtype)], axis=1)


def _attn_a_lat_kernel(sink_ref, q_ref, kp_ref, kc_ref, kn_ref, kx_ref,
                       vp_ref, vc_ref, vn_ref, vx_ref, o_ref, bufa, bufb):
    g = pl.program_id(0)
    nqb = SEQ // QB_A
    nsub = QB_A // WBLK
    rows = A_GROUP * WBLK

    @pl.when(g == 0)
    def _():
        for r in bufb:
            r[...] = jnp.zeros(r.shape, F32)

    def scores(buf):
        sl_ref, sx_ref, m_ref, d_ref = buf
        t = jnp.minimum(g, B * A_KV * nqb - 1)
        kvh = (t // nqb) % A_KV
        qb = t % nqb
        kfull = jnp.concatenate([kp_ref[...], kc_ref[...], kn_ref[...]], axis=0)
        kx = kx_ref[...]
        sink = _sink_col(sink_ref, kvh, WBLK)
        ti = lax.broadcasted_iota(jnp.int32, (rows, 3 * WBLK), 0) % WBLK
        kj = lax.broadcasted_iota(jnp.int32, (rows, 3 * WBLK), 1)
        rel = kj - ti
        band = jnp.logical_and(rel >= 0, rel <= 2 * WBLK)
        first = qb == 0
        last = qb == nqb - 1
        for s in range(nsub):
            qs = _stack_heads(q_ref[s * WBLK:(s + 1) * WBLK, :])
            kw = kfull[s * WBLK:(s + 3) * WBLK, :]
            valid = band
            if s == 0:
                valid = jnp.logical_and(valid, jnp.logical_not(jnp.logical_and(kj < WBLK, first)))
            if s == nsub - 1:
                valid = jnp.logical_and(
                    valid, jnp.logical_not(jnp.logical_and(kj >= 2 * WBLK, last)))
            s_loc = jnp.where(valid, _nt_dot(qs, kw), -1e30)
            s_ctx = _nt_dot(qs, kx)
            m = jnp.maximum(jnp.maximum(jnp.max(s_loc, axis=-1, keepdims=True),
                                        jnp.max(s_ctx, axis=-1, keepdims=True)), sink)
            sl_ref[s] = s_loc
            sx_ref[s] = s_ctx
            m_ref[s] = m
            d_ref[s] = jnp.exp2(sink - m)

    def outputs(buf):
        sl_ref, sx_ref, m_ref, d_ref = buf
        vfull = _with_ones(jnp.concatenate([vp_ref[...], vc_ref[...], vn_ref[...]], axis=0))
        vx = _with_ones(vx_ref[...])
        for s in range(nsub):
            vw = vfull[s * WBLK:(s + 3) * WBLK, :]
            m = m_ref[s][:, :1]
            p_loc = jnp.exp2(sl_ref[s] - m).astype(BF16)
            p_ctx = jnp.exp2(sx_ref[s] - m).astype(BF16)
            res = (jnp.dot(p_loc, vw, preferred_element_type=F32)
                   + jnp.dot(p_ctx, vx, preferred_element_type=F32))
            o = res[:, :HD] / (res[:, HD:] + d_ref[s])
            o_ref[s * WBLK:(s + 1) * WBLK, :] = _unstack_heads(o, WBLK).astype(BF16)

    @pl.when(g % 2 == 0)
    def _():
        scores(bufa)
        outputs(bufb)

    @pl.when(g % 2 == 1)
    def _():
        scores(bufb)
        outputs(bufa)


def _attn_a_lat(sink, proj, proj_ctx, col0):
    nqb = SEQ // QB_A
    sub = QB_A // WBLK
    nblk = SEQ // WBLK
    n = B * A_KV * nqb
    cur = lambda g: jnp.minimum(g, n - 1)
    prv = lambda g: jnp.maximum(g - 1, 0)
    bat = lambda t: t // (A_KV * nqb)
    kvh = lambda t: (t // nqb) % A_KV
    qb = lambda t: t % nqb
    row = lambda t: bat(t) * nqb + qb(t)
    prev = lambda t: bat(t) * nblk + jnp.maximum(qb(t) * sub - 1, 0)
    nxt = lambda t: bat(t) * nblk + jnp.minimum(qb(t) * sub + sub, nblk - 1)
    rows = A_GROUP * WBLK

    def kv_specs(col, ctx_col, at):
        return [
            pl.BlockSpec((WBLK, HD), lambda g: (prev(at(g)), col // HD + kvh(at(g)))),
            pl.BlockSpec((QB_A, HD), lambda g: (row(at(g)), col // HD + kvh(at(g)))),
            pl.BlockSpec((WBLK, HD), lambda g: (nxt(at(g)), col // HD + kvh(at(g)))),
            pl.BlockSpec((CTX, HD), lambda g: (bat(at(g)), ctx_col // HD + kvh(at(g)))),
        ]

    def buf():
        return [pltpu.VMEM((sub, rows, 3 * WBLK), F32), pltpu.VMEM((sub, rows, CTX), F32),
                pltpu.VMEM((sub, rows, HD), F32), pltpu.VMEM((sub, rows, HD), F32)]

    return pl.pallas_call(
        _attn_a_lat_kernel,
        grid=(n + 1,),
        in_specs=[
            pl.BlockSpec(memory_space=pltpu.SMEM),
            pl.BlockSpec((QB_A, A_GROUP * HD),
                         lambda g: (row(cur(g)), AQ // (A_GROUP * HD) + kvh(cur(g)))),
            *kv_specs(AK, AK - col0, cur),
            *kv_specs(AV, AV - col0, prv),
        ],
        out_specs=pl.BlockSpec((QB_A, A_GROUP * HD), lambda g: (row(prv(g)), kvh(prv(g)))),
        out_shape=jax.ShapeDtypeStruct((N_LAT, W1K), BF16),
        scratch_shapes=[buf(), buf()],
        compiler_params=_cparams(("arbitrary",)),
        name="attn_a_lat",
    )(sink, proj, proj, proj, proj, proj_ctx, proj, proj, proj, proj_ctx)


def _attn_a_ctx_kernel(sink_ref, q_ref, kx_ref, vx_ref, o_ref):
    kvh = pl.program_id(1)
    qs = _stack_heads(q_ref[...])
    sink = _sink_col(sink_ref, kvh, CTX)
    s_ctx = _nt_dot(qs, kx_ref[...])
    m = jnp.maximum(jnp.max(s_ctx, axis=-1, keepdims=True), sink)
    p = jnp.exp2(s_ctx - m[:, :1]).astype(BF16)
    res = jnp.dot(p, _with_ones(vx_ref[...]), preferred_element_type=F32)
    o = res[:, :HD] / (res[:, HD:] + jnp.exp2(sink - m))
    o_ref[...] = _unstack_heads(o, CTX).astype(BF16)


def _attn_a_ctx(sink, proj_ctx):
    return pl.pallas_call(
        _attn_a_ctx_kernel,
        grid=(B, A_KV),
        in_specs=[
            pl.BlockSpec(memory_space=pltpu.SMEM),
            pl.BlockSpec((CTX, A_GROUP * HD), lambda b, k: (b, AQ // (A_GROUP * HD) + k)),
            pl.BlockSpec((CTX, HD), lambda b, k: (b, AK // HD + k)),
            pl.BlockSpec((CTX, HD), lambda b, k: (b, AV // HD + k)),
        ],
        out_specs=pl.BlockSpec((CTX, A_GROUP * HD), lambda b, k: (b, k)),
        out_shape=jax.ShapeDtypeStruct((N_CTX, W1K), BF16),
        compiler_params=_cparams(("parallel", "parallel")),
        name="attn_a_ctx",
    )(sink, proj_ctx, proj_ctx, proj_ctx)


def _attn_b_lat_kernel(q_ref, kl_ref, kx_ref, vl_ref, vx_ref, o_ref,
                       sa_ref, ma_ref, sb_ref, mb_ref):
    g = pl.program_id(0)

    @pl.when(g == 0)
    def _():
        sb_ref[...] = jnp.zeros(sb_ref.shape, F32)
        mb_ref[...] = jnp.zeros(mb_ref.shape, F32)

    def step(s_cur, m_cur, s_prev, m_prev):
        q = q_ref[...]
        s1 = _nt_dot(q, kl_ref[...])
        s2 = _nt_dot(q, kx_ref[...])
        s_cur[:, :SEQ] = s1
        s_cur[:, SEQ:] = s2
        m_cur[...] = jnp.maximum(jnp.max(s1, axis=-1, keepdims=True),
                                 jnp.max(s2, axis=-1, keepdims=True))
        m = m_prev[...]
        p1 = jnp.exp2(s_prev[:, :SEQ] - m).astype(BF16)
        p2 = jnp.exp2(s_prev[:, SEQ:] - m).astype(BF16)
        res = (jnp.dot(p1, vl_ref[...], preferred_element_type=F32)
               + jnp.dot(p2, vx_ref[...], preferred_element_type=F32))
        o_ref[...] = (res[:, :VD] / res[:, VD:]).astype(BF16)

    @pl.when(g % 2 == 0)
    def _():
        step(sa_ref, ma_ref, sb_ref, mb_ref)

    @pl.when(g % 2 == 1)
    def _():
        step(sb_ref, mb_ref, sa_ref, ma_ref)


def _attn_b_lat(qcat, kcat, vcat, kcat_ctx, vcat_ctx):
    nq = SEQ // TQ_B
    per_b = B_HEADS * nq
    n = B * per_b
    cur = lambda g: jnp.minimum(g, n - 1)
    prv = lambda g: jnp.maximum(g - 1, 0)
    bat = lambda t: t // per_b
    head = lambda t: (t // nq) % B_HEADS
    qrow = lambda t: bat(t) * nq + t % nq
    return pl.pallas_call(
        _attn_b_lat_kernel,
        grid=(n + 1,),
        in_specs=[
            pl.BlockSpec((TQ_B, 2 * HD), lambda g: (qrow(cur(g)), head(cur(g)))),
            pl.BlockSpec((SEQ, 2 * HD), lambda g: (bat(cur(g)), head(cur(g)))),
            pl.BlockSpec((CTX, 2 * HD), lambda g: (bat(cur(g)), head(cur(g)))),
            pl.BlockSpec((SEQ, 2 * HD), lambda g: (bat(prv(g)), head(prv(g)))),
            pl.BlockSpec((CTX, 2 * HD), lambda g: (bat(prv(g)), head(prv(g)))),
        ],
        out_specs=pl.BlockSpec((TQ_B, VD), lambda g: (qrow(prv(g)), head(prv(g)))),
        out_shape=jax.ShapeDtypeStruct((N_LAT, W1K), BF16),
        scratch_shapes=[pltpu.VMEM((TQ_B, SEQ + CTX), F32), pltpu.VMEM((TQ_B, 1), F32),
                        pltpu.VMEM((TQ_B, SEQ + CTX), F32), pltpu.VMEM((TQ_B, 1), F32)],
        compiler_params=_cparams(("arbitrary",)),
        name="attn_b_lat",
    )(qcat, kcat, kcat_ctx, vcat, vcat_ctx)


def _attn_b_ctx_kernel(q_ref, kx_ref, vx_ref, o_ref):
    for h in range(B_HEADS):
        hs = slice(h * 2 * HD, (h + 1) * 2 * HD)
        s = _nt_dot(q_ref[:, hs], kx_ref[:, hs])
        p = jnp.exp2(s - jnp.max(s, axis=-1, keepdims=True)).astype(BF16)
        res = jnp.dot(p, vx_ref[:, hs], preferred_element_type=F32)
        o_ref[:, h * VD:(h + 1) * VD] = (res[:, :VD] / res[:, VD:]).astype(BF16)


def _attn_b_ctx(qcat, kcat, vcat):
    blk = pl.BlockSpec((CTX, 2 * W1K), lambda b: (b, 0))
    return pl.pallas_call(
        _attn_b_ctx_kernel,
        grid=(B,),
        in_specs=[blk, blk, blk],
        out_specs=pl.BlockSpec((CTX, W1K), lambda b: (b, 0)),
        out_shape=jax.ShapeDtypeStruct((N_CTX, W1K), BF16),
        compiler_params=_cparams(("parallel",)),
        name="attn_b_ctx",
    )(qcat, kcat, vcat)


def _merge_kernel(ya_ref, yb_ref, za_ref, zb_ref, cu_ref, cv_ref, cz_ref, ga_ref, gb_ref, gc_ref,
                  x_ref, gate_ref, lng_ref, lnb_ref, ws_ref, bs_ref,
                  wpa_ref, wpb_ref, wpc_ref, wout_ref, fg_ref, o_ref, m0_ref, m1_ref, *, final):
    g = pl.program_id(0)
    tm = x_ref.shape[0]
    nchunk = tm // CHUNK

    @pl.when(g == 0)
    def _():
        m1_ref[...] = jnp.zeros(m1_ref.shape, BF16)

    def silu(z_ref):
        z = z_ref[...].astype(F32)
        return z * jax.nn.sigmoid(z)

    def branch(y, z_ref, g_ref, w_ref):
        t = (y * silu(z_ref)).astype(BF16)
        return jax.nn.sigmoid(g_ref[...].astype(F32)) * jnp.dot(
            t, w_ref[...], preferred_element_type=F32)

    def step(m_cur, m_prev):
        out = jnp.dot(m_prev[...], wout_ref[...], preferred_element_type=F32)
        xn = x_ref[...] + gate_ref[...] * out
        if final:
            ms = jnp.mean(xn * xn, axis=-1, keepdims=True)
            xn = xn * lax.rsqrt(ms + EPS) * fg_ref[...]
        o_ref[...] = xn

        v = cv_ref[...].astype(F32)
        mu = jnp.mean(v, axis=-1, keepdims=True)
        vc = v - mu
        var = jnp.mean(vc * vc, axis=-1, keepdims=True)
        vn = (vc * lax.rsqrt(var + EPS) * lng_ref[...] + lnb_ref[...]).astype(BF16)
        cols = []
        for grp in range(C_GROUPS):
            gs = slice(grp * CHUNK, (grp + 1) * CHUNK)
            rhs = jnp.concatenate(
                [vn[c * CHUNK:(c + 1) * CHUNK, gs] for c in range(nchunk)], axis=1)
            mx = jnp.dot(ws_ref[grp], rhs, preferred_element_type=F32)
            cols.append(jnp.concatenate(
                [mx[:, c * CHUNK:(c + 1) * CHUNK] for c in range(nchunk)], axis=0))
        bias = jnp.concatenate([bs_ref[...]] * nchunk, axis=0)
        yc = cu_ref[...].astype(F32) * (jnp.concatenate(cols, axis=1) + bias)
        m = (branch(ya_ref[...].astype(F32), za_ref, ga_ref, wpa_ref)
             + branch(yb_ref[...].astype(F32), zb_ref, gb_ref, wpb_ref)
             + branch(yc, cz_ref, gc_ref, wpc_ref))
        m_cur[...] = m.astype(BF16)

    @pl.when(g % 2 == 0)
    def _():
        step(m0_ref, m1_ref)

    @pl.when(g % 2 == 1)
    def _():
        step(m1_ref, m0_ref)


def _merge(x_rows, gate, group0, y_a, y_b, proj, lng, lnb, ws, bs, wpa, wpb, wpc, wout, fg, final):
    tm = TM_MERGE
    n = x_rows.shape[0] // tm
    cur = lambda g: jnp.minimum(g, n - 1)
    prv = lambda g: jnp.maximum(g - 1, 0)
    seg = lambda c: pl.BlockSpec((tm, W1K), lambda g: (cur(g), c // W1K))
    seg2 = lambda c: pl.BlockSpec((tm, D), lambda g: (cur(g), c // D))
    const = lambda shape: pl.BlockSpec(shape, lambda g: (0,) * len(shape),
                                       pipeline_mode=pl.Buffered(1))
    return pl.pallas_call(
        functools.partial(_merge_kernel, final=final),
        grid=(n + 1,),
        in_specs=[
            seg(0), seg(0), seg(AZ), seg(BZ), seg(CU), seg(CV), seg(CZ),
            seg2(GA), seg2(GB), seg2(GC),
            pl.BlockSpec((tm, D), lambda g: (prv(g), 0)),
            pl.BlockSpec((None, 1, D), lambda g: (group0 + prv(g) // (SEQ // tm), 0, 0)),
            const((1, W1K)), const((1, W1K)),
            const((C_GROUPS, CHUNK, CHUNK)), const((CHUNK, W1K)),
            const((W1K, D)), const((W1K, D)), const((W1K, D)), const((D, D)),
            const((1, D)),
        ],
        out_specs=pl.BlockSpec((tm, D), lambda g: (prv(g), 0)),
        out_shape=jax.ShapeDtypeStruct(x_rows.shape, F32),
        scratch_shapes=[pltpu.VMEM((tm, D), BF16), pltpu.VMEM((tm, D), BF16)],
        compiler_params=_cparams(("arbitrary",)),
        name="merge_final" if final else "merge",
    )(y_a, y_b, proj, proj, proj, proj, proj, proj, proj, proj, x_rows, gate,
      lng, lnb, ws, bs, wpa, wpb, wpc, wout, fg)


def _rope_tables(width):
    t = np.arange(SEQ)
    row = (t // GRID_W).astype(np.float64)
    col = (t % GRID_W).astype(np.float64)
    half = width // 4
    lane = np.arange(128)
    u = lane % (width // 2)
    inv = THETA ** (-(u % half).astype(np.float64) / half)
    pos = np.where((lane // (width // 2))[None, :] == 0, row[:, None], col[:, None])
    ang = pos * inv[None, :]
    active = (lane < width)[None, :]
    first = (u < half)[None, :]
    cos = np.where(active, np.cos(ang), 1.0)
    sin = np.where(active, np.sin(ang), 0.0)
    tab = np.stack([cos, np.where(first, -sin, 0.0), np.where(first, 0.0, sin)])
    return jnp.asarray(tab, dtype=F32)


PIECE = 512
N_ALIGNED = 7
N_PIECES = 27
KR_COL = N_ALIGNED * PIECE


def _prep_w_kernel(a_ref, b_ref, o_ref):
    p = pl.program_id(1)

    @pl.when(p < N_ALIGNED)
    def _():
        o_ref[...] = a_ref[...].astype(BF16)

    @pl.when(p >= N_ALIGNED)
    def _():
        o_ref[:PIECE - ROPE, :] = a_ref[ROPE:, :].astype(BF16)
        o_ref[PIECE - ROPE:, :] = b_ref[...].astype(BF16)


def _prep_w_in(w_in_t):
    def dst(p):
        d = jnp.where(p >= 3, p - 1, p)
        d = jnp.where(p >= 5, p + 20, d)
        d = jnp.where(p == 2, AK // PIECE, d)
        return jnp.where(p >= N_ALIGNED, p - 3, d)

    return pl.pallas_call(
        _prep_w_kernel,
        grid=(DEPTH, N_PIECES),
        in_specs=[
            pl.BlockSpec((None, PIECE, D), lambda l, p: (l, p, 0)),
            pl.BlockSpec((None, ROPE, D), lambda l, p: (l, (p + 1) * (PIECE // ROPE), 0)),
        ],
        out_specs=pl.BlockSpec((None, PIECE, D), lambda l, p: (l, dst(p), 0)),
        out_shape=jax.ShapeDtypeStruct((DEPTH, NPROJ, D), BF16),
        compiler_params=_cparams(("parallel", "parallel")),
        name="prep_w_in",
    )(w_in_t, w_in_t)


def kernel(x, c, ctx, c_ctx, ada_w, ada_b, norm_g, w_in, sink_a, mla_gq, mla_gkv, w_uq, w_ukv,
           sgu_ln_g, sgu_ln_b, sgu_w, sgu_b, w_pa, w_pb, w_pc, w_out, final_g):
    assert x.shape == (B, SEQ, D) and ctx.shape == (B, CTX, D)
    x_lat = x.reshape(N_LAT, D)
    x_ctx = ctx.reshape(N_CTX, D)
    cc = jnp.concatenate([c, c_ctx[None, :], jnp.zeros((8 - B - 1, D), F32)], axis=0)
    mod = _adaln(cc, ada_w, ada_b)
    t128 = _rope_tables(HD)
    t64 = _rope_tables(ROPE)
    w_in_t = jnp.swapaxes(w_in, 1, 2)
    w_t = _prep_w_in(w_in_t)

    for l in range(DEPTH):
        final = l == DEPTH - 1
        shift = mod[l, :, 0:D].reshape(8, 1, D)
        scale = mod[l, :, D:2 * D].reshape(8, 1, D)
        gate = mod[l, :, 2 * D:].reshape(8, 1, D)
        wq = jnp.pad(w_uq[l].reshape(LORA, B_HEADS, NOPE + ROPE),
                     ((0, 0), (0, 0), (0, 2 * HD - NOPE - ROPE))).reshape(LORA, 2 * W1K).astype(BF16)
        wkv = w_ukv[l].reshape(LORA, B_HEADS, 2, HD).transpose(0, 2, 1, 3).reshape(
            LORA, 2 * W1K).astype(BF16)
        bs = jnp.repeat(sgu_b[l].T, CHUNK, axis=1)
        ng = norm_g[l].reshape(1, D)
        gq = mla_gq[l].reshape(1, LORA)
        gkv = mla_gkv[l].reshape(1, LORA)
        merge_w = (sgu_ln_g[l].reshape(1, W1K), sgu_ln_b[l].reshape(1, W1K),
                   sgu_w[l].astype(BF16), bs,
                   w_pa[l].astype(BF16), w_pb[l].astype(BF16), w_pc[l].astype(BF16),
                   w_out[l].astype(BF16), final_g.reshape(1, D))

        col0 = AK if final else 0
        proj_c, kr_c = _inproj(x_ctx, ng, scale, shift, B, w_t, w_in_t, l, None, col0)
        ctx_kv = _mla_proj(proj_c, col0, kr_c, gq, gkv, wq, wkv, None, not final)
        kcat_c, vcat_c = ctx_kv[-2:]

        proj, kr = _inproj(x_lat, ng, scale, shift, 0, w_t, w_in_t, l, (t128, t64), 0)
        qcat, kcat, vcat = _mla_proj(proj, 0, kr, gq, gkv, wq, wkv, t64, True)
        y_a = _attn_a_lat(sink_a[l], proj, proj_c, col0)
        y_b = _attn_b_lat(qcat, kcat, vcat, kcat_c, vcat_c)
        if not final:
            x_ctx = _merge(x_ctx, gate, B, _attn_a_ctx(sink_a[l], proj_c),
                           _attn_b_ctx(ctx_kv[0], kcat_c, vcat_c), proj_c, *merge_w, False)
        x_lat = _merge(x_lat, gate, 0, y_a, y_b, proj, *merge_w, final)
    return x_lat.reshape(B, SEQ, D)
```
